```python
import math
import jax, jax.numpy as jnp
from jax import lax
import numpy as np

D_MODEL = 2048
BATCH = 2
SEQ = 16384
DEPTH = 4
DEC_BATCH = 2
DEC_SEQ = 4096
PAST_LEN = 128

DA_HEADS = 4
DA_DIM = 64
DA_VDIM = 2 * DA_DIM
DA_WIDTH = DA_HEADS * DA_VDIM
ROPE_THETA = 500000.0
ROPE_DIMS = DA_DIM // 4
Q_BLOCK = 128
HG_HEADS = 4
HG_DIM = 128
HG_WIDTH = HG_HEADS * HG_DIM
HG_CHUNK = 64
SSM_HEADS = 16
SSM_HEADDIM = 64
SSM_WIDTH = SSM_HEADS * SSM_HEADDIM
SSM_GROUPS = 2
SSM_HPG = SSM_HEADS // SSM_GROUPS
SSM_STATE = 128
SSM_CONV = 5
SSM_CHUNK = 64
SSM_CONV_CH = SSM_WIDTH + 2 * SSM_GROUPS * SSM_STATE
MIX_WIDTH = DA_WIDTH + HG_WIDTH + SSM_WIDTH
PROJ_WIDTH = 4 * DA_WIDTH + 5 * HG_WIDTH + SSM_WIDTH + SSM_CONV_CH + 2 * SSM_HEADS
EPS = 1e-6

kernel_name = 'hybrid_bidir_diffattn_hgrn2_ssd_encoder'


def _split_points():
    sizes = (DA_WIDTH,) * 4 + (HG_WIDTH,) * 5 + (SSM_WIDTH, SSM_CONV_CH, 2 * SSM_HEADS)
    pts, acc = [], 0
    for s in sizes[:-1]:
        acc += s
        pts.append(acc)
    return pts


def rmsnorm(x, w):
    xf = x.astype(jnp.float32)
    y = xf * lax.rsqrt(jnp.mean(xf * xf, axis=-1, keepdims=True) + EPS)
    return (y * w.astype(jnp.float32)).astype(x.dtype)


def rope_tables(seq):
    inv = ROPE_THETA ** (-jnp.arange(0, ROPE_DIMS, 2, dtype=jnp.float32) / ROPE_DIMS)
    ang = jnp.arange(seq, dtype=jnp.float32)[:, None] * inv[None, :]
    return jnp.cos(ang), jnp.sin(ang)


def partial_rope(t, cos, sin):
    half = ROPE_DIMS // 2
    tf = t.astype(jnp.float32)
    c = cos[None, :, None, None, :]
    s = sin[None, :, None, None, :]
    t1, t2 = tf[..., :half], tf[..., half:ROPE_DIMS]
    out = jnp.concatenate([t1 * c - t2 * s, t2 * c + t1 * s, tf[..., ROPE_DIMS:]], axis=-1)
    return out.astype(t.dtype)


def diff_attention(q, k, v, lam):
    Bn, H, _, L, d = q.shape
    nb = L // Q_BLOCK
    scale = d ** -0.5
    qb = jnp.moveaxis(q.reshape(Bn, H, 2, nb, Q_BLOCK, d), 3, 0)

    def block(qi):
        s = jnp.einsum('bhiqd,bhikd->bhiqk', qi, k).astype(jnp.float32) * scale
        p = jax.nn.softmax(s, axis=-1)
        w = p[:, :, 0] - lam * p[:, :, 1]
        return jnp.einsum('bhqk,bhke->bhqe', w.astype(v.dtype), v)

    o = lax.map(block, qb)
    return jnp.transpose(o, (1, 0, 3, 2, 4)).reshape(Bn, L, H, 2 * d)


def segment_decay(cum):
    T = cum.shape[-1]
    mask = jnp.tril(jnp.ones((T, T), dtype=bool))
    return jnp.exp(jnp.where(mask, cum[..., :, None] - cum[..., None, :], -jnp.inf))


def hgrn2_direction(q, k, v, g):
    Bn, H, L, dk = q.shape
    dv = v.shape[-1]
    nc = L // HG_CHUNK

    def chunks(t):
        return jnp.moveaxis(t.astype(jnp.float32).reshape(Bn, H, nc, HG_CHUNK, t.shape[-1]), 2, 0)

    mask = jnp.tril(jnp.ones((HG_CHUNK, HG_CHUNK), dtype=bool))[:, :, None]

    def step(S, inp):
        qc, kc, vc, gc = inp
        b = jnp.cumsum(gc, axis=-2)
        dec = jnp.exp(jnp.where(mask, b[..., :, None, :] - b[..., None, :, :], -jnp.inf))
        attn = jnp.einsum('bhtd,bhsd,bhtsd->bhts', qc, kc, dec)
        o = jnp.einsum('bhts,bhse->bhte', attn, vc) + jnp.einsum('bhtd,bhde->bhte', qc * jnp.exp(b), S)
        b_last = b[..., -1:, :]
        S = jnp.swapaxes(jnp.exp(b_last), -1, -2) * S + jnp.einsum('bhsd,bhse->bhde', kc * jnp.exp(b_last - b), vc)
        return S, o

    S0 = jnp.zeros((Bn, H, dk, dv), jnp.float32)
    _, o = lax.scan(step, S0, (chunks(q), chunks(k), chunks(v), chunks(g)))
    return jnp.moveaxis(o, 0, 2).reshape(Bn, H, L, dv)


def hgrn2_mixer(qr, fr_f, fr_b, ir, lb, norm_w):
    Bn, L, _ = qr.shape

    def heads(t):
        return jnp.transpose(t.reshape(Bn, L, HG_HEADS, HG_DIM), (0, 2, 1, 3))

    q = jax.nn.silu(heads(qr))
    v = heads(ir)
    lb_h = lb.astype(jnp.float32).reshape(HG_HEADS, 1, HG_DIM)

    def gates(fr):
        g = jnp.logaddexp(jnp.log(lb_h), jnp.log1p(-lb_h) + jax.nn.log_sigmoid(heads(fr).astype(jnp.float32)))
        return g, 1.0 - jnp.exp(g)

    g_f, k_f = gates(fr_f)
    g_b, k_b = gates(fr_b)
    fl = lambda t: jnp.flip(t, axis=2)
    o = hgrn2_direction(q, k_f, v, g_f) + fl(hgrn2_direction(fl(q), fl(k_b), fl(v), fl(g_b)))
    o = jnp.transpose(o, (0, 2, 1, 3)).astype(qr.dtype)
    return rmsnorm(o, norm_w).reshape(Bn, L, HG_WIDTH)


def ssd_direction(x, a, Bm, Cm):
    Bn, L, G, R, P = x.shape
    N = Bm.shape[-1]
    T = SSM_CHUNK
    nc = L // T
    x = x.reshape(Bn, nc, T, G, R, P)
    Bm = Bm.astype(jnp.float32).reshape(Bn, nc, T, G, N)
    Cm = Cm.astype(jnp.float32).reshape(Bn, nc, T, G, N)
    a = jnp.transpose(a.reshape(Bn, nc, T, G, R), (0, 3, 4, 1, 2))
    a_cum = jnp.cumsum(a, axis=-1)
    Lmat = segment_decay(a_cum)
    cb = jnp.einsum('bclgn,bcsgn->bcgls', Cm, Bm)
    y_diag = jnp.einsum('bcgls,bgrcls,bcsgrp->bclgrp', cb, Lmat, x)
    decay_states = jnp.exp(a_cum[..., -1:] - a_cum)
    states = jnp.einsum('bclgn,bgrcl,bclgrp->bcgrpn', Bm, decay_states, x)
    a_tot = a_cum[..., -1]
    cum_pad = jnp.concatenate([jnp.zeros_like(a_tot[..., :1]), jnp.cumsum(a_tot, axis=-1)], axis=-1)
    decay_chunk = segment_decay(cum_pad)
    states_pad = jnp.concatenate([jnp.zeros_like(states[:, :1]), states], axis=1)
    prev = jnp.einsum('bgrzc,bcgrpn->bzgrpn', decay_chunk, states_pad)[:, :-1]
    y_off = jnp.einsum('bclgn,bcgrpn,bgrcl->bclgrp', Cm, prev, jnp.exp(a_cum))
    return (y_diag + y_off).reshape(Bn, L, G, R, P)


def mamba2_mixer(z, xbc, dt_raw, conv_w, conv_b, A_log, dt_bias, D_skip, norm_w):
    Bn, L, _ = z.shape
    G, R, P, N = SSM_GROUPS, SSM_HPG, SSM_HEADDIM, SSM_STATE
    xbc = lax.conv_general_dilated(xbc, conv_w[:, None, :], window_strides=(1,),
                                   padding=[(SSM_CONV // 2, SSM_CONV // 2)],
                                   dimension_numbers=('NWC', 'WIO', 'NWC'),
                                   feature_group_count=SSM_CONV_CH)
    xbc = jax.nn.silu(xbc + conv_b)
    xs = xbc[..., :SSM_WIDTH].reshape(Bn, L, G, R, P)
    Bm = xbc[..., SSM_WIDTH:SSM_WIDTH + G * N].reshape(Bn, L, G, N)
    Cm = xbc[..., SSM_WIDTH + G * N:].reshape(Bn, L, G, N)
    dt = jax.nn.softplus(dt_raw.astype(jnp.float32).reshape(Bn, L, 2, SSM_HEADS) + dt_bias.astype(jnp.float32))
    A = -jnp.exp(A_log.astype(jnp.float32))
    xs32 = xs.astype(jnp.float32)

    def direction(d, reverse):
        dt_d = dt[:, :, d].reshape(Bn, L, G, R)
        a = dt_d * A[d].reshape(G, R)
        xd = xs32 * dt_d[..., None]
        if reverse:
            fl = lambda t: jnp.flip(t, axis=1)
            return fl(ssd_direction(fl(xd), fl(a), fl(Bm), fl(Cm)))
        return ssd_direction(xd, a, Bm, Cm)

    y = direction(0, False) + direction(1, True) + D_skip.astype(jnp.float32).reshape(G, R)[..., None] * xs32
    y = y.reshape(Bn, L, SSM_WIDTH).astype(z.dtype) * jax.nn.silu(z)
    y = rmsnorm(y.reshape(Bn, L, G, SSM_WIDTH // G), norm_w.reshape(G, SSM_WIDTH // G))
    return y.reshape(Bn, L, SSM_WIDTH)


def mixer_layer(x, layer_idx, cos, sin, lb, norm_w, w_in, w_out, diff_lambda, diff_norm_w,
                hgrn_norm_w, conv_w, conv_b, A_log, dt_bias, D_skip, ssm_norm_w):
    Bn, L, _ = x.shape
    h = rmsnorm(x, norm_w)
    proj = jnp.einsum('bld,de->ble', h, w_in)
    (qa, ka, va, ga, qb, ffb, fbb, ib, gb, zc, xbc, dtc) = jnp.split(proj, _split_points(), axis=-1)

    q = jnp.transpose(partial_rope(qa.reshape(Bn, L, DA_HEADS, 2, DA_DIM), cos, sin), (0, 2, 3, 1, 4))
    k = jnp.transpose(partial_rope(ka.reshape(Bn, L, DA_HEADS, 2, DA_DIM), cos, sin), (0, 2, 3, 1, 4))
    v = jnp.transpose(va.reshape(Bn, L, DA_HEADS, DA_VDIM), (0, 2, 1, 3))
    lam_init = 0.8 - 0.6 * math.exp(-0.3 * layer_idx)
    dl = diff_lambda.astype(jnp.float32)
    lam = jnp.exp(jnp.sum(dl[0] * dl[1])) - jnp.exp(jnp.sum(dl[2] * dl[3])) + lam_init
    oa = diff_attention(q, k, v, lam)
    oa = (rmsnorm(oa, diff_norm_w) * (1.0 - lam_init)).reshape(Bn, L, DA_WIDTH) * jax.nn.silu(ga)

    ob = hgrn2_mixer(qb, ffb, fbb, ib, lb, hgrn_norm_w) * jax.nn.silu(gb)

    oc = mamba2_mixer(zc, xbc, dtc, conv_w, conv_b, A_log, dt_bias, D_skip, ssm_norm_w)

    mixed = jnp.concatenate([oa, ob.astype(oa.dtype), oc.astype(oa.dtype)], axis=-1)
    return x + jnp.einsum('ble,ed->bld', mixed, w_out)


def trunk(x, norm_w, w_in, w_out, diff_lambda, diff_norm_w, hgrn_lower_bounds, hgrn_norm_w,
          conv_w, conv_b, ssm_A_log, ssm_dt_bias, ssm_D, ssm_norm_w, final_norm_w):
    L = x.shape[1]
    cos, sin = rope_tables(L)
    lb_all = jnp.cumsum(jax.nn.softmax(hgrn_lower_bounds.astype(jnp.float32), axis=0), axis=0)
    lb_all = lb_all - lb_all[0]
    for i in range(DEPTH):
        x = mixer_layer(x, i, cos, sin, lb_all[i], norm_w[i], w_in[i], w_out[i], diff_lambda[i],
                        diff_norm_w[i], hgrn_norm_w[i], conv_w[i], conv_b[i], ssm_A_log[i],
                        ssm_dt_bias[i], ssm_D[i], ssm_norm_w[i])
    return rmsnorm(x, final_norm_w)


def setup_inputs(seed: int = 0) -> dict:
    key = jax.random.key(seed)
    ks = jax.random.split(key, 18)
    nrm = jax.random.normal
    x_prompt = nrm(ks[0], (BATCH, SEQ, D_MODEL), jnp.float32)
    x_sample = nrm(ks[1], (DEC_BATCH, DEC_SEQ, D_MODEL), jnp.float32)
    norm_w = 1.0 + 0.02 * nrm(ks[2], (DEPTH, D_MODEL), jnp.float32)
    w_in = nrm(ks[3], (DEPTH, D_MODEL, PROJ_WIDTH), jnp.float32) * D_MODEL ** -0.5
    w_out = nrm(ks[4], (DEPTH, MIX_WIDTH, D_MODEL), jnp.float32) * MIX_WIDTH ** -0.5
    diff_lambda = 0.1 * nrm(ks[5], (DEPTH, 4, DA_DIM), jnp.float32)
    diff_norm_w = 1.0 + 0.02 * nrm(ks[6], (DEPTH, DA_VDIM), jnp.float32)
    hgrn_lower_bounds = 0.5 * nrm(ks[7], (DEPTH, HG_WIDTH), jnp.float32)
    hgrn_norm_w = 1.0 + 0.02 * nrm(ks[8], (DEPTH, HG_DIM), jnp.float32)
    conv_w = nrm(ks[9], (DEPTH, SSM_CONV, SSM_CONV_CH), jnp.float32) * SSM_CONV ** -0.5
    conv_b = 0.02 * nrm(ks[10], (DEPTH, SSM_CONV_CH), jnp.float32)
    ssm_A_log = jnp.log(jax.random.uniform(ks[11], (DEPTH, 2, SSM_HEADS), jnp.float32, 1.0, 16.0))
    dt0 = jnp.exp(jax.random.uniform(ks[12], (DEPTH, 2, SSM_HEADS), jnp.float32, math.log(1e-3), math.log(1e-1)))
    ssm_dt_bias = dt0 + jnp.log(-jnp.expm1(-dt0))
    ssm_D = 1.0 + 0.02 * nrm(ks[13], (DEPTH, SSM_HEADS), jnp.float32)
    ssm_norm_w = 1.0 + 0.02 * nrm(ks[14], (DEPTH, SSM_WIDTH), jnp.float32)
    final_norm_w = 1.0 + 0.02 * nrm(ks[15], (D_MODEL,), jnp.float32)
    return {'x_prompt': x_prompt, 'x_sample': x_sample, 'norm_w': norm_w, 'w_in': w_in, 'w_out': w_out,
            'diff_lambda': diff_lambda, 'diff_norm_w': diff_norm_w, 'hgrn_lower_bounds': hgrn_lower_bounds,
            'hgrn_norm_w': hgrn_norm_w, 'conv_w': conv_w, 'conv_b': conv_b, 'ssm_A_log': ssm_A_log,
            'ssm_dt_bias': ssm_dt_bias, 'ssm_D': ssm_D, 'ssm_norm_w': ssm_norm_w, 'final_norm_w': final_norm_w}


def reference(x_prompt, x_sample, norm_w, w_in, w_out, diff_lambda, diff_norm_w, hgrn_lower_bounds,
              hgrn_norm_w, conv_w, conv_b, ssm_A_log, ssm_dt_bias, ssm_D, ssm_norm_w, final_norm_w):
    y_prompt = trunk(x_prompt, norm_w, w_in, w_out, diff_lambda, diff_norm_w, hgrn_lower_bounds, hgrn_norm_w,
                     conv_w, conv_b, ssm_A_log, ssm_dt_bias, ssm_D, ssm_norm_w, final_norm_w)
    y_sample = trunk(x_sample, norm_w, w_in, w_out, diff_lambda, diff_norm_w, hgrn_lower_bounds, hgrn_norm_w,
                     conv_w, conv_b, ssm_A_log, ssm_dt_bias, ssm_D, ssm_norm_w, final_norm_w)
    return (y_prompt, y_sample)
```

```python
import functools
import math

import jax
import jax.numpy as jnp
from jax import lax
from jax.experimental import pallas as pl
from jax.experimental.pallas import tpu as pltpu

F32 = jnp.float32
BF16 = jnp.bfloat16

D_MODEL = 2048
DEPTH = 4
DA_HEADS = 4
DA_DIM = 64
DA_VDIM = 128
DA_WIDTH = 512
ROPE_THETA = 500000.0
ROPE_DIMS = 16
HG_HEADS = 4
HG_DIM = 128
HG_WIDTH = 512
SSM_HEADS = 16
SSM_HEADDIM = 64
SSM_WIDTH = 1024
SSM_GROUPS = 2
SSM_HPG = 8
SSM_STATE = 128
SSM_CONV = 5
SSM_CONV_CH = 1536
MIX_WIDTH = 2048
PROJ_WIDTH = 7200
EPS = 1e-6

LANE = 128
SUBLANE = 8
PROJ_PAD = 7680
C_QA, C_KA, C_VA, C_GA = 0, 4, 8, 12
C_QB, C_FF, C_FB, C_IB, C_GB = 16, 20, 24, 28, 32
C_ZC, C_XBC, C_DT = 36, 44, 56

VMEM_LIMIT = 56 * 1024 * 1024

HG_T = 256
HG_LEVELS = (128, 64, 32, 16, 8)
SSD_T = 128
NEG_BIG = -1e30


def _cparams(sem):
    return pltpu.CompilerParams(dimension_semantics=sem, vmem_limit_bytes=VMEM_LIMIT)


def _sigmoid(x):
    return 1.0 / (1.0 + jnp.exp(-x))


def _silu(x):
    return x * _sigmoid(x)


def _log1pexp_negabs(x):
    return jnp.log1p(jnp.exp(-jnp.abs(x)))


def _softplus(x):
    return jnp.maximum(x, 0.0) + _log1pexp_negabs(x)


def _dot(a, b):
    return jnp.dot(a, b, preferred_element_type=F32)


def _dot_nt(a, b):
    return lax.dot_general(a, b, (((1,), (1,)), ((), ())), preferred_element_type=F32)


def _split(x, parts):
    out = []
    r = x
    for i in range(parts):
        p = r.astype(BF16)
        out.append(p)
        if i + 1 < parts:
            r = r - p.astype(F32)
    return out


def _sel_dot(sel, x, parts=3):
    ps = _split(x, parts)
    acc = _dot(sel, ps[-1])
    for p in ps[-2::-1]:
        acc = acc + _dot(sel, p)
    return acc


def _dot_sel(x, sel, parts=3):
    ps = _split(x, parts)
    acc = _dot(ps[-1], sel)
    for p in ps[-2::-1]:
        acc = acc + _dot(p, sel)
    return acc


def _inproj_kernel(x_ref, nw_ref, w_ref, o_ref, h_scr):
    @pl.when(pl.program_id(1) == 0)
    def _():
        x = x_ref[...]
        ms = jnp.mean(x * x, axis=-1, keepdims=True)
        h_scr[...] = (x * lax.rsqrt(ms + EPS) * nw_ref[...]).astype(BF16)

    o_ref[...] = _dot(h_scr[...], w_ref[...])


def _inproj(x2, nw, w_bf16, tm=1024, tn=768):
    m = x2.shape[0]
    tm = min(tm, m)
    return pl.pallas_call(
        _inproj_kernel,
        grid=(m // tm, PROJ_PAD // tn),
        in_specs=[
            pl.BlockSpec((tm, D_MODEL), lambda i, j: (i, 0)),
            pl.BlockSpec((1, D_MODEL), lambda i, j: (0, 0)),
            pl.BlockSpec((D_MODEL, tn), lambda i, j: (0, j)),
        ],
        out_specs=pl.BlockSpec((tm, tn), lambda i, j: (i, j)),
        out_shape=jax.ShapeDtypeStruct((m, PROJ_PAD), F32),
        scratch_shapes=[pltpu.VMEM((tm, D_MODEL), BF16)],
        compiler_params=_cparams(("parallel", "arbitrary")),
        name="inproj",
    )(x2, nw, w_bf16)


def _rope_tables(seq):
    half = ROPE_DIMS // 2
    inv = ROPE_THETA ** (-jnp.arange(0, ROPE_DIMS, 2, dtype=F32) / ROPE_DIMS)
    ang = jnp.arange(seq, dtype=F32)[:, None] * inv[None, :]
    cos, sin = jnp.cos(ang), jnp.sin(ang)
    ones = jnp.ones((seq, DA_DIM - ROPE_DIMS), F32)
    zeros = jnp.zeros((seq, DA_DIM - ROPE_DIMS), F32)
    zh = jnp.zeros((seq, half), F32)
    c64 = jnp.concatenate([cos, cos, ones], axis=1)
    s1_64 = jnp.concatenate([-sin, zh, zeros], axis=1)
    s2_64 = jnp.concatenate([zh, sin, zeros], axis=1)
    return (jnp.concatenate([c64, c64], axis=1), jnp.concatenate([s1_64, s1_64], axis=1),
            jnp.concatenate([s2_64, s2_64], axis=1))


def _prep_kernel(q_ref, k_ref, v_ref, c_ref, s1_ref, s2_ref, qo_ref, ko_ref, vo_ref):
    c, s1, s2 = c_ref[...], s1_ref[...], s2_ref[...]
    scale = DA_DIM ** -0.5
    for h in range(DA_HEADS):
        sl = slice(h * LANE, (h + 1) * LANE)
        for src, dst, mul in ((q_ref, qo_ref, scale), (k_ref, ko_ref, None)):
            t = src[:, sl]
            r = t * c + pltpu.roll(t, LANE - ROPE_DIMS // 2, 1) * s1 + pltpu.roll(t, ROPE_DIMS // 2, 1) * s2
            if mul is not None:
                r = r * mul
            dst[:, sl] = r.astype(BF16)
    vo_ref[...] = v_ref[...].astype(BF16)


def _prep(proj, tabs, seq, tr=512):
    m = proj.shape[0]
    tr = min(tr, seq)
    nb = seq // tr
    w = DA_WIDTH
    tab_spec = pl.BlockSpec((tr, LANE), lambda i: (i % nb, 0))
    out = jax.ShapeDtypeStruct((m, w), BF16)
    return pl.pallas_call(
        _prep_kernel,
        grid=(m // tr,),
        in_specs=[
            pl.BlockSpec((tr, w), lambda i: (i, C_QA * LANE // w)),
            pl.BlockSpec((tr, w), lambda i: (i, C_KA * LANE // w)),
            pl.BlockSpec((tr, w), lambda i: (i, C_VA * LANE // w)),
            tab_spec, tab_spec, tab_spec,
        ],
        out_specs=[pl.BlockSpec((tr, w), lambda i: (i, 0))] * 3,
        out_shape=[out, out, out],
        compiler_params=_cparams(("parallel",)),
        name="rope_prep",
    )(proj, proj, proj, *tabs)


def _attn_kernel(lam_ref, q_ref, k_ref, v_ref, g_ref, nw_ref, o_ref, qs_scr, m_scr, acc_scr, *, tq, nk):
    j = pl.program_id(3)

    @pl.when(j == 0)
    def _():
        q = q_ref[...]
        lane = lax.broadcasted_iota(jnp.int32, q.shape, 1)
        zero = jnp.zeros_like(q)
        qs_scr[0:tq, :] = jnp.where(lane < DA_DIM, q, zero)
        qs_scr[tq:2 * tq, :] = jnp.where(lane >= DA_DIM, q, zero)
        m_scr[...] = jnp.full(m_scr.shape, -jnp.inf, F32)
        acc_scr[...] = jnp.zeros(acc_scr.shape, F32)

    s = _dot_nt(qs_scr[...], k_ref[...])
    m_prev = m_scr[...]
    m_new = jnp.maximum(m_prev, jnp.max(s, axis=1, keepdims=True))
    alpha = jnp.exp(m_prev - m_new)
    p = jnp.exp(s - m_new).astype(BF16)
    v = v_ref[...]
    vext = jnp.concatenate([v, jnp.ones_like(v)], axis=1)
    acc_scr[...] = alpha * acc_scr[...] + _dot(p, vext)
    m_scr[...] = m_new

    @pl.when(j == nk - 1)
    def _():
        a = acc_scr[...]
        o1 = a[0:tq, 0:LANE] / a[0:tq, LANE:LANE + 1]
        o2 = a[tq:2 * tq, 0:LANE] / a[tq:2 * tq, LANE:LANE + 1]
        o = o1 - lam_ref[0] * o2
        ms = jnp.mean(o * o, axis=-1, keepdims=True)
        o = o * lax.rsqrt(ms + EPS) * nw_ref[...] * lam_ref[1]
        o_ref[...] = (o * _silu(g_ref[...])).astype(BF16)


def _attention(lamv, q_r, k_r, v_b, proj, nw, bsz, seq, tq=512, tk=512):
    m = q_r.shape[0]
    tq, tk = min(tq, seq), min(tk, seq)
    nq, nk = seq // tq, seq // tk
    return pl.pallas_call(
        functools.partial(_attn_kernel, tq=tq, nk=nk),
        grid=(bsz, DA_HEADS, nq, nk),
        in_specs=[
            pl.BlockSpec(memory_space=pltpu.SMEM),
            pl.BlockSpec((tq, LANE), lambda b, h, i, j: (b * nq + i, h)),
            pl.BlockSpec((tk, LANE), lambda b, h, i, j: (b * nk + j, h)),
            pl.BlockSpec((tk, LANE), lambda b, h, i, j: (b * nk + j, h)),
            pl.BlockSpec((tq, LANE), lambda b, h, i, j: (b * nq + i, C_GA + h)),
            pl.BlockSpec((1, LANE), lambda b, h, i, j: (0, 0)),
        ],
        out_specs=pl.BlockSpec((tq, LANE), lambda b, h, i, j: (b * nq + i, h)),
        out_shape=jax.ShapeDtypeStruct((m, DA_WIDTH), BF16),
        scratch_shapes=[
            pltpu.VMEM((2 * tq, LANE), BF16),
            pltpu.VMEM((2 * tq, 1), F32),
            pltpu.VMEM((2 * tq, 2 * LANE), F32),
        ],
        compiler_params=_cparams(("parallel", "parallel", "parallel", "arbitrary")),
        name="diff_attn",
    )(lamv, q_r, k_r, v_b, proj, nw)


def _hgrn_consts(rev):
    t = HG_T
    r = jnp.arange(t)[:, None]
    c = jnp.arange(t)[None, :]
    tri = (c >= r) if rev else (c <= r)
    masks = []
    for hh in HG_LEVELS:
        same = (r // (2 * hh)) == (c // (2 * hh))
        r_hi = (r % (2 * hh)) >= hh
        c_hi = (c % (2 * hh)) >= hh
        mk = same & ((~r_hi) & c_hi if rev else r_hi & (~c_hi))
        masks.append(mk)
    return tri.astype(BF16), jnp.stack(masks).astype(F32), jnp.ones((LANE, LANE), BF16)


def _hgrn_kernel(*refs, rev):
    if rev:
        (q_ref, f_ref, v_ref, la_ref, lc_ref, tri_ref, msk_ref, ones_ref,
         of_ref, gb_ref, nw_ref, o_ref, st_scr) = refs
    else:
        q_ref, f_ref, v_ref, la_ref, lc_ref, tri_ref, msk_ref, ones_ref, o_ref, st_scr = refs
    t = HG_T

    @pl.when(pl.program_id(2) == 0)
    def _():
        st_scr[...] = jnp.zeros(st_scr.shape, F32)

    q = _silu(q_ref[...])
    fr = f_ref[...]
    v = v_ref[...]
    y = lc_ref[...] + (jnp.minimum(fr, 0.0) - _log1pexp_negabs(fr))
    la = la_ref[...]
    g = jnp.maximum(la, y) + _log1pexp_negabs(la - y)
    f = jnp.exp(g)
    k = 1.0 - f
    cum = _sel_dot(tri_ref[...], g)
    tot = cum[0:1, :] if rev else cum[t - 1:t, :]
    qb, kb, vb = q.astype(BF16), k.astype(BF16), v.astype(BF16)

    o = _dot_nt((q * jnp.exp(cum)).astype(BF16), st_scr[...].astype(BF16))

    amat = jnp.zeros((t, t), F32)
    for li, hh in enumerate(HG_LEVELS):
        nb = t // (2 * hh)
        cum3 = cum.reshape(nb, 2 * hh, LANE)
        idx = hh if rev else hh - 1
        ref = jnp.broadcast_to(cum3[:, idx:idx + 1, :], cum3.shape).reshape(t, LANE)
        qt = q * jnp.exp(jnp.minimum(cum - ref, 0.0))
        kt = k * jnp.exp(jnp.minimum(ref - cum, 0.0))
        amat = amat + msk_ref[li] * _dot_nt(qt.astype(BF16), kt.astype(BF16))
    o = o + _dot(amat.astype(BF16), vb)

    row8 = lax.broadcasted_iota(jnp.int32, (t, LANE), 0) & (SUBLANE - 1)
    ones = ones_ref[...]
    o = o + _dot((q * k).astype(BF16), ones) * v
    prod = None
    for d in range(1, SUBLANE):
        sh = (t - d) if rev else d
        if d == 1:
            prod = f
        else:
            prod = prod * pltpu.roll(f, (t - (d - 1)) if rev else (d - 1), 0)
        valid = (row8 + d <= SUBLANE - 1) if rev else (row8 >= d)
        term = jnp.where(valid, q * pltpu.roll(k, sh, 0) * prod, 0.0)
        o = o + _dot(term.astype(BF16), ones) * pltpu.roll(v, sh, 0)

    kh = (k * jnp.exp(tot - cum)).astype(BF16)
    st_scr[...] = st_scr[...] * jnp.exp(tot) + _dot(v.T.astype(BF16), kh)

    if rev:
        o = o + of_ref[...]
        ms = jnp.mean(o * o, axis=-1, keepdims=True)
        o = o * lax.rsqrt(ms + EPS) * nw_ref[...]
        o_ref[...] = (o * _silu(gb_ref[...])).astype(BF16)
    else:
        o_ref[...] = o


def _hgrn(proj, log_lb, log1m_lb, nw, bsz, seq):
    m = proj.shape[0]
    t = HG_T
    nc = seq // t
    outs = None
    for rev in (False, True):
        tri, msk, ones = _hgrn_consts(rev)
        if rev:
            rowmap = lambda b, h, c: b * nc + (nc - 1 - c)
        else:
            rowmap = lambda b, h, c: b * nc + c

        def col(c0, rowmap=rowmap):
            return pl.BlockSpec((t, LANE), lambda b, h, c: (rowmap(b, h, c), c0 + h))

        lane_row = pl.BlockSpec((1, LANE), lambda b, h, c: (0, h))
        const2 = lambda shp: pl.BlockSpec(shp, lambda b, h, c: (0,) * len(shp))
        in_specs = [col(C_QB), col(C_FB if rev else C_FF), col(C_IB), lane_row, lane_row,
                    const2((t, t)), const2((len(HG_LEVELS), t, t)), const2((LANE, LANE))]
        args = [proj, proj, proj, log_lb, log1m_lb, tri, msk, ones]
        if rev:
            in_specs += [col(0), col(C_GB), pl.BlockSpec((1, LANE), lambda b, h, c: (0, 0))]
            args += [outs, proj, nw]
        outs = pl.pallas_call(
            functools.partial(_hgrn_kernel, rev=rev),
            grid=(bsz, HG_HEADS, nc),
            in_specs=in_specs,
            out_specs=col(0),
            out_shape=jax.ShapeDtypeStruct((m, HG_WIDTH), BF16 if rev else F32),
            scratch_shapes=[pltpu.VMEM((HG_DIM, HG_DIM), F32)],
            compiler_params=_cparams(("parallel", "parallel", "arbitrary")),
            name="hgrn_bwd" if rev else "hgrn_fwd",
        )(*args)
    return outs


def _conv_kernel(x_ref, p_ref, n_ref, w_ref, b_ref, o_ref, ext_scr, *, tl, nb):
    i = pl.program_id(0)
    first = (i % nb) == 0
    last = (i % nb) == nb - 1
    halo = SUBLANE
    ext_scr[0:halo, :] = jnp.where(first, 0.0, p_ref[...])
    ext_scr[halo:halo + tl, :] = x_ref[...]
    ext_scr[halo + tl:2 * halo + tl, :] = jnp.where(last, 0.0, n_ref[...])
    acc = jnp.broadcast_to(b_ref[...], (tl, x_ref.shape[1]))
    pad = SSM_CONV // 2
    for jj in range(SSM_CONV):
        acc = acc + w_ref[jj:jj + 1, :] * ext_scr[halo - pad + jj:halo - pad + jj + tl, :]
    o_ref[...] = _silu(acc)


def _conv(proj, w, b, seq, tl=512):
    m = proj.shape[0]
    tl = min(tl, seq)
    nb = seq // tl
    cw = 512
    c0 = C_XBC * LANE // cw
    r8 = tl // SUBLANE
    last8 = m // SUBLANE - 1
    return pl.pallas_call(
        functools.partial(_conv_kernel, tl=tl, nb=nb),
        grid=(m // tl, SSM_CONV_CH // cw),
        in_specs=[
            pl.BlockSpec((tl, cw), lambda i, c: (i, c0 + c)),
            pl.BlockSpec((SUBLANE, cw), lambda i, c: (jnp.maximum(i * r8 - 1, 0), c0 + c)),
            pl.BlockSpec((SUBLANE, cw), lambda i, c: (jnp.minimum((i + 1) * r8, last8), c0 + c)),
            pl.BlockSpec((SSM_CONV, cw), lambda i, c: (0, c)),
            pl.BlockSpec((1, cw), lambda i, c: (0, c)),
        ],
        out_specs=pl.BlockSpec((tl, cw), lambda i, c: (i, c)),
        out_shape=jax.ShapeDtypeStruct((m, SSM_CONV_CH), F32),
        scratch_shapes=[pltpu.VMEM((tl + 2 * SUBLANE, cw), F32)],
        compiler_params=_cparams(("parallel", "parallel")),
        name="ssm_conv",
    )(proj, proj, proj, w, b)


def _ssd_consts(rev):
    t = SSD_T
    r = jnp.arange(t)[:, None]
    c = jnp.arange(t)[None, :]
    tri = ((c >= r) if rev else (c <= r)).astype(BF16)
    d = 1 if rev else 0
    lane = jnp.arange(LANE)[:, None]
    colhead = (jnp.arange(SSM_HPG * SSM_HEADDIM) // SSM_HEADDIM)[None, :]
    xm = jnp.stack([(lane == d * SSM_HEADS + g * SSM_HPG + colhead) for g in range(SSM_GROUPS)]).astype(BF16)
    return tri, xm


def _ssd_kernel(*refs, rev):
    if rev:
        (xs_ref, bc_ref, dt_ref, dtb_ref, arow_ref, tri_ref, xm_ref,
         yf_ref, z0_ref, z1_ref, drow_ref, nw_ref, o_ref, st_scr) = refs
    else:
        xs_ref, bc_ref, dt_ref, dtb_ref, arow_ref, tri_ref, xm_ref, o_ref, st_scr = refs
    t = SSD_T
    d = 1 if rev else 0
    gw = SSM_HPG * SSM_HEADDIM

    @pl.when(pl.program_id(1) == 0)
    def _():
        st_scr[...] = jnp.zeros(st_scr.shape, F32)

    dt = _softplus(dt_ref[...] + dtb_ref[...])
    a = dt * arow_ref[...]
    cum = _sel_dot(tri_ref[...], a)
    cum_t = cum.T
    r_i = lax.broadcasted_iota(jnp.int32, (t, t), 0)
    c_i = lax.broadcasted_iota(jnp.int32, (t, t), 1)
    tmask = (c_i >= r_i) if rev else (c_i <= r_i)
    lane_lo = lax.broadcasted_iota(jnp.int32, (t, LANE), 1) < SSM_HEADDIM
    last = 0 if rev else t - 1

    for g in range(SSM_GROUPS):
        xm = xm_ref[g]
        e_cum = _dot_sel(cum, xm, 3)
        e_dt = _dot_sel(dt, xm, 2)
        e_tot = e_cum[last:last + 1, :]
        xs = xs_ref[:, g * gw:(g + 1) * gw]
        bm = bc_ref[:, g * LANE:(g + 1) * LANE]
        cm = bc_ref[:, (SSM_GROUPS + g) * LANE:(SSM_GROUPS + g + 1) * LANE]
        xd = xs * e_dt
        xdb = xd.astype(BF16)
        cmb = cm.astype(BF16)
        cb = _dot_nt(cmb, bm.astype(BF16))
        ys = []
        for p in range(SSM_HPG // 2):
            xp = xdb[:, p * LANE:(p + 1) * LANE]
            yy = []
            for jj in range(2):
                ln = d * SSM_HEADS + g * SSM_HPG + 2 * p + jj
                diff = cum[:, ln:ln + 1] - cum_t[ln:ln + 1, :]
                w = cb * jnp.exp(jnp.where(tmask, diff, NEG_BIG))
                yy.append(_dot(w.astype(BF16), xp))
            ys.append(jnp.where(lane_lo, yy[0], yy[1]))
        y = jnp.concatenate(ys, axis=1)
        y = y + _dot(cmb, st_scr[g].astype(BF16)) * jnp.exp(e_cum)
        xdec = (xd * jnp.exp(e_tot - e_cum)).astype(BF16)
        st_scr[g] = st_scr[g] * jnp.exp(e_tot) + _dot(bm.T.astype(BF16), xdec)

        if rev:
            sl = slice(g * gw, (g + 1) * gw)
            y = y + yf_ref[:, sl] + drow_ref[:, sl] * xs
            z = (z1_ref if g else z0_ref)[...]
            y = y * _silu(z)
            ms = jnp.mean(y * y, axis=-1, keepdims=True)
            o_ref[:, sl] = (y * lax.rsqrt(ms + EPS) * nw_ref[:, sl]).astype(BF16)
        else:
            o_ref[:, g * gw:(g + 1) * gw] = y


def _ssd(proj, xc, dtb, arow, drow, nw, bsz, seq):
    m = proj.shape[0]
    t = SSD_T
    nc = seq // t
    outs = None
    for rev in (False, True):
        tri, xm = _ssd_consts(rev)
        if rev:
            rowmap = lambda b, c: b * nc + (nc - 1 - c)
        else:
            rowmap = lambda b, c: b * nc + c

        def blk(width, cidx, rowmap=rowmap):
            return pl.BlockSpec((t, width), lambda b, c: (rowmap(b, c), cidx))

        const2 = lambda shp: pl.BlockSpec(shp, lambda b, c: (0,) * len(shp))
        in_specs = [blk(SSM_WIDTH, 0), blk(512, 2), blk(LANE, C_DT), const2((1, LANE)), const2((1, LANE)),
                    const2((t, t)), const2((SSM_GROUPS, LANE, 512))]
        args = [xc, xc, proj, dtb, arow, tri, xm]
        if rev:
            zc0 = C_ZC * LANE // 512
            in_specs += [blk(SSM_WIDTH, 0), blk(512, zc0), blk(512, zc0 + 1),
                         const2((1, SSM_WIDTH)), const2((1, SSM_WIDTH))]
            args += [outs, proj, proj, drow, nw]
        outs = pl.pallas_call(
            functools.partial(_ssd_kernel, rev=rev),
            grid=(bsz, nc),
            in_specs=in_specs,
            out_specs=blk(SSM_WIDTH, 0),
            out_shape=jax.ShapeDtypeStruct((m, SSM_WIDTH), BF16 if rev else F32),
            scratch_shapes=[pltpu.VMEM((SSM_GROUPS, SSM_STATE, 512), F32)],
            compiler_params=_cparams(("parallel", "arbitrary")),
            name="ssd_bwd" if rev else "ssd_fwd",
        )(*args)
    return outs


def _outproj_kernel(x_ref, a_ref, b_ref, c_ref, w_ref, o_ref):
    acc = _dot(a_ref[...], w_ref[0:DA_WIDTH, :])
    acc = acc + _dot(b_ref[...], w_ref[DA_WIDTH:DA_WIDTH + HG_WIDTH, :])
    acc = acc + _dot(c_ref[...], w_ref[DA_WIDTH + HG_WIDTH:MIX_WIDTH, :])
    o_ref[...] = x_ref[...] + acc


def _outproj(x2, oa, ob, oc, w_bf16, tm=512):
    m = x2.shape[0]
    tm = min(tm, m)
    return pl.pallas_call(
        _outproj_kernel,
        grid=(m // tm,),
        in_specs=[
            pl.BlockSpec((tm, D_MODEL), lambda i: (i, 0)),
            pl.BlockSpec((tm, DA_WIDTH), lambda i: (i, 0)),
            pl.BlockSpec((tm, HG_WIDTH), lambda i: (i, 0)),
            pl.BlockSpec((tm, SSM_WIDTH), lambda i: (i, 0)),
            pl.BlockSpec((MIX_WIDTH, D_MODEL), lambda i: (0, 0)),
        ],
        out_specs=pl.BlockSpec((tm, D_MODEL), lambda i: (i, 0)),
        out_shape=jax.ShapeDtypeStruct((m, D_MODEL), F32),
        compiler_params=_cparams(("parallel",)),
        name="outproj",
    )(x2, oa, ob, oc, w_bf16)


def _norm_kernel(x_ref, w_ref, o_ref):
    x = x_ref[...]
    ms = jnp.mean(x * x, axis=-1, keepdims=True)
    o_ref[...] = x * lax.rsqrt(ms + EPS) * w_ref[...]


def _final_norm(x2, w, tm=512):
    m = x2.shape[0]
    tm = min(tm, m)
    return pl.pallas_call(
        _norm_kernel,
        grid=(m // tm,),
        in_specs=[pl.BlockSpec((tm, D_MODEL), lambda i: (i, 0)), pl.BlockSpec((1, D_MODEL), lambda i: (0, 0))],
        out_specs=pl.BlockSpec((tm, D_MODEL), lambda i: (i, 0)),
        out_shape=jax.ShapeDtypeStruct((m, D_MODEL), F32),
        compiler_params=_cparams(("parallel",)),
        name="final_norm",
    )(x2, w)


def _layer_params(p, i, lb_all):
    lam_init = 0.8 - 0.6 * math.exp(-0.3 * i)
    dl = p["diff_lambda"][i].astype(F32)
    lam = jnp.exp(jnp.sum(dl[0] * dl[1])) - jnp.exp(jnp.sum(dl[2] * dl[3])) + lam_init
    lb = lb_all[i]
    pad32 = lambda v: jnp.pad(v.reshape(1, -1), ((0, 0), (0, LANE - 2 * SSM_HEADS)))
    return dict(
        norm_w=p["norm_w"][i].reshape(1, -1),
        w_in=p["w_in_bf16"][i],
        w_out=p["w_out_bf16"][i],
        lamv=jnp.stack([lam, jnp.asarray(1.0 - lam_init, F32)]).astype(F32),
        diff_nw=p["diff_norm_w"][i].reshape(1, -1),
        log_lb=jnp.log(lb).reshape(1, -1),
        log1m_lb=jnp.log1p(-lb).reshape(1, -1),
        hgrn_nw=p["hgrn_norm_w"][i].reshape(1, -1),
        conv_w=p["conv_w"][i],
        conv_b=p["conv_b"][i].reshape(1, -1),
        dtb=pad32(p["ssm_dt_bias"][i].astype(F32)),
        arow=pad32(-jnp.exp(p["ssm_A_log"][i].astype(F32))),
        drow=jnp.repeat(p["ssm_D"][i].astype(F32), SSM_HEADDIM).reshape(1, -1),
        ssm_nw=p["ssm_norm_w"][i].reshape(1, -1),
    )


def _trunk(x, p, layers):
    bsz, seq, _ = x.shape
    x2 = x.reshape(bsz * seq, D_MODEL)
    tabs = _rope_tables(seq)
    for lp in layers:
        proj = _inproj(x2, lp["norm_w"], lp["w_in"])
        q_r, k_r, v_b = _prep(proj, tabs, seq)
        oa = _attention(lp["lamv"], q_r, k_r, v_b, proj, lp["diff_nw"], bsz, seq)
        ob = _hgrn(proj, lp["log_lb"], lp["log1m_lb"], lp["hgrn_nw"], bsz, seq)
        xc = _conv(proj, lp["conv_w"], lp["conv_b"], seq)
        oc = _ssd(proj, xc, lp["dtb"], lp["arow"], lp["drow"], lp["ssm_nw"], bsz, seq)
        x2 = _outproj(x2, oa, ob, oc, lp["w_out"])
    y = _final_norm(x2, p["final_norm_w"].reshape(1, -1))
    return y.reshape(bsz, seq, D_MODEL)


def kernel(x_prompt, x_sample, norm_w, w_in, w_out, diff_lambda, diff_norm_w, hgrn_lower_bounds, hgrn_norm_w,
           conv_w, conv_b, ssm_A_log, ssm_dt_bias, ssm_D, ssm_norm_w, final_norm_w):
    p = dict(norm_w=norm_w, diff_lambda=diff_lambda, diff_norm_w=diff_norm_w, hgrn_norm_w=hgrn_norm_w,
             conv_w=conv_w, conv_b=conv_b, ssm_A_log=ssm_A_log, ssm_dt_bias=ssm_dt_bias, ssm_D=ssm_D,
             ssm_norm_w=ssm_norm_w, final_norm_w=final_norm_w)
    p["w_in_bf16"] = jnp.pad(w_in, ((0, 0), (0, 0), (0, PROJ_PAD - PROJ_WIDTH))).astype(BF16)
    p["w_out_bf16"] = w_out.astype(BF16)
    lb_all = jnp.cumsum(jax.nn.softmax(hgrn_lower_bounds.astype(F32), axis=0), axis=0)
    lb_all = lb_all - lb_all[0]
    layers = [_layer_params(p, i, lb_all) for i in range(DEPTH)]
    return (_trunk(x_prompt, p, layers), _trunk(x_sample, p, layers))
```

```python
import functools
import math

import jax
import jax.numpy as jnp
from jax import lax
from jax.experimental import pallas as pl
from jax.experimental.pallas import tpu as pltpu

F32 = jnp.float32
BF16 = jnp.bfloat16

D_MODEL = 2048
DEPTH = 4
DA_HEADS = 4
DA_DIM = 64
DA_VDIM = 128
DA_WIDTH = 512
ROPE_THETA = 500000.0
ROPE_DIMS = 16
HG_HEADS = 4
HG_DIM = 128
HG_WIDTH = 512
SSM_HEADS = 16
SSM_HEADDIM = 64
SSM_WIDTH = 1024
SSM_GROUPS = 2
SSM_HPG = 8
SSM_STATE = 128
SSM_CONV = 5
SSM_CONV_CH = 1536
MIX_WIDTH = 2048
PROJ_WIDTH = 7200
EPS = 1e-6

LANE = 128
SUBLANE = 8
PROJ_PAD = 7680
C_QA, C_KA, C_VA, C_GA = 0, 4, 8, 12
C_QB, C_FF, C_FB, C_IB, C_GB = 16, 20, 24, 28, 32
C_ZC, C_XBC, C_DT = 36, 44, 56

VMEM_LIMIT = 56 * 1024 * 1024

HG_T = 256
HG_LEVELS = (128, 64, 32, 16, 8)
SSD_T = 128
NEG_BIG = -1e30
LOG2E = 1.4426950408889634
V_ROWS = 144
ATTN_CW = 256
ATTN_AHEAD = 3


def _cparams(sem):
    return pltpu.CompilerParams(dimension_semantics=sem, vmem_limit_bytes=VMEM_LIMIT)


def _sigmoid(x):
    return 1.0 / (1.0 + jnp.exp(-x))


def _silu(x):
    return x * _sigmoid(x)


def _log1pexp_negabs(x):
    return jnp.log1p(jnp.exp(-jnp.abs(x)))


def _softplus(x):
    return jnp.maximum(x, 0.0) + _log1pexp_negabs(x)


def _dot(a, b):
    return jnp.dot(a, b, preferred_element_type=F32)


def _dot_nt(a, b):
    return lax.dot_general(a, b, (((1,), (1,)), ((), ())), preferred_element_type=F32)


def _split(x, parts):
    out = []
    r = x
    for i in range(parts):
        p = r.astype(BF16)
        out.append(p)
        if i + 1 < parts:
            r = r - p.astype(F32)
    return out


def _sel_dot(sel, x, parts=3):
    ps = _split(x, parts)
    acc = _dot(sel, ps[-1])
    for p in ps[-2::-1]:
        acc = acc + _dot(sel, p)
    return acc


def _dot_sel(x, sel, parts=3):
    ps = _split(x, parts)
    acc = _dot(ps[-1], sel)
    for p in ps[-2::-1]:
        acc = acc + _dot(p, sel)
    return acc


def _inproj_kernel(x_ref, nw_ref, w_ref, o_ref, h_scr):
    @pl.when(pl.program_id(1) == 0)
    def _():
        x = x_ref[...]
        ms = jnp.mean(x * x, axis=-1, keepdims=True)
        h_scr[...] = (x * lax.rsqrt(ms + EPS) * nw_ref[...]).astype(BF16)

    o_ref[...] = _dot(h_scr[...], w_ref[...])


def _inproj(x2, nw, w_bf16, tm=1024, tn=768):
    m = x2.shape[0]
    tm = min(tm, m)
    return pl.pallas_call(
        _inproj_kernel,
        grid=(m // tm, PROJ_PAD // tn),
        in_specs=[
            pl.BlockSpec((tm, D_MODEL), lambda i, j: (i, 0)),
            pl.BlockSpec((1, D_MODEL), lambda i, j: (0, 0)),
            pl.BlockSpec((D_MODEL, tn), lambda i, j: (0, j)),
        ],
        out_specs=pl.BlockSpec((tm, tn), lambda i, j: (i, j)),
        out_shape=jax.ShapeDtypeStruct((m, PROJ_PAD), F32),
        scratch_shapes=[pltpu.VMEM((tm, D_MODEL), BF16)],
        compiler_params=_cparams(("parallel", "arbitrary")),
        name="inproj",
    )(x2, nw, w_bf16)


def _rope_tables(seq):
    half = ROPE_DIMS // 2
    inv = ROPE_THETA ** (-jnp.arange(0, ROPE_DIMS, 2, dtype=F32) / ROPE_DIMS)
    ang = jnp.arange(seq, dtype=F32)[:, None] * inv[None, :]
    cos, sin = jnp.cos(ang), jnp.sin(ang)
    ones = jnp.ones((seq, DA_DIM - ROPE_DIMS), F32)
    zeros = jnp.zeros((seq, DA_DIM - ROPE_DIMS), F32)
    zh = jnp.zeros((seq, half), F32)
    c64 = jnp.concatenate([cos, cos, ones], axis=1)
    s1_64 = jnp.concatenate([-sin, zh, zeros], axis=1)
    s2_64 = jnp.concatenate([zh, sin, zeros], axis=1)
    return (jnp.concatenate([c64, c64], axis=1), jnp.concatenate([s1_64, s1_64], axis=1),
            jnp.concatenate([s2_64, s2_64], axis=1))


def _prep_kernel(q_ref, k_ref, v_ref, c_ref, s1_ref, s2_ref, qo_ref, ko_ref, vo_ref):
    c, s1, s2 = c_ref[...], s1_ref[...], s2_ref[...]
    qmul = DA_DIM ** -0.5 * LOG2E
    tr = q_ref.shape[0]
    for h in range(DA_HEADS):
        sl = slice(h * LANE, (h + 1) * LANE)
        for src, is_q in ((q_ref, True), (k_ref, False)):
            t = src[:, sl]
            r = t * c + pltpu.roll(t, LANE - ROPE_DIMS // 2, 1) * s1 + pltpu.roll(t, ROPE_DIMS // 2, 1) * s2
            if is_q:
                qo_ref[sl, :] = (r * qmul).T.astype(BF16)
            else:
                ko_ref[:, sl] = r.astype(BF16)
        vo_ref[h * V_ROWS:h * V_ROWS + DA_VDIM, :] = v_ref[:, sl].T.astype(BF16)
        vo_ref[h * V_ROWS + DA_VDIM:(h + 1) * V_ROWS, :] = jnp.ones((V_ROWS - DA_VDIM, tr), BF16)


def _prep(proj, tabs, seq, tr=512):
    m = proj.shape[0]
    tr = min(tr, seq)
    nb = seq // tr
    w = DA_WIDTH
    tab_spec = pl.BlockSpec((tr, LANE), lambda i: (i % nb, 0))
    return pl.pallas_call(
        _prep_kernel,
        grid=(m // tr,),
        in_specs=[
            pl.BlockSpec((tr, w), lambda i: (i, C_QA * LANE // w)),
            pl.BlockSpec((tr, w), lambda i: (i, C_KA * LANE // w)),
            pl.BlockSpec((tr, w), lambda i: (i, C_VA * LANE // w)),
            tab_spec, tab_spec, tab_spec,
        ],
        out_specs=[pl.BlockSpec((w, tr), lambda i: (0, i)),
                   pl.BlockSpec((tr, w), lambda i: (i, 0)),
                   pl.BlockSpec((DA_HEADS * V_ROWS, tr), lambda i: (0, i))],
        out_shape=[jax.ShapeDtypeStruct((w, m), BF16), jax.ShapeDtypeStruct((m, w), BF16),
                   jax.ShapeDtypeStruct((DA_HEADS * V_ROWS, m), BF16)],
        compiler_params=_cparams(("parallel",)),
        name="rope_prep",
    )(proj, proj, proj, *tabs)


def _attn_kernel(lam_ref, q_ref, k_ref, v_ref, g_ref, nw_ref, o_ref, qs_scr, m_scr, acc_scr, s_scr, *, tq, tk, nkb):
    q = q_ref[...]
    row = lax.broadcasted_iota(jnp.int32, q.shape, 0)
    zero = jnp.zeros_like(q)
    qs_scr[:, 0:tq] = jnp.where(row < DA_DIM, q, zero)
    qs_scr[:, tq:2 * tq] = jnp.where(row >= DA_DIM, q, zero)
    m_scr[...] = jnp.full(m_scr.shape, -jnp.inf, F32)
    acc_scr[...] = jnp.zeros(acc_scr.shape, F32)
    ncc = 2 * tq // ATTN_CW

    def scores(kb, c):
        kblk = k_ref[pl.ds(pl.multiple_of(kb * tk, tk), tk), :]
        return _dot(kblk, qs_scr[:, c * ATTN_CW:(c + 1) * ATTN_CW])

    for c in range(ATTN_AHEAD):
        s_scr[c] = scores(0, c)

    def body(kb, carry):
        vt = v_ref[:, pl.ds(pl.multiple_of(kb * tk, tk), tk)]
        kb_next = jnp.minimum(kb + 1, nkb - 1)
        for c in range(ncc):
            cs = slice(c * ATTN_CW, (c + 1) * ATTN_CW)
            ca = c + ATTN_AHEAD
            s_scr[ca % ncc] = scores(kb, ca) if ca < ncc else scores(kb_next, ca - ncc)
            st = s_scr[c]
            m_prev = m_scr[:, cs]
            m_new = jnp.maximum(m_prev, jnp.max(st, axis=0, keepdims=True))
            alpha = jnp.exp2(m_prev - m_new)
            pt = jnp.exp2(st - m_new).astype(BF16)
            acc_scr[:, cs] = acc_scr[:, cs] * alpha + _dot(vt, pt)
            m_scr[:, cs] = m_new
        return carry

    lax.fori_loop(0, nkb, body, 0)

    a = acc_scr[...]
    o = a[0:DA_VDIM, :] / a[DA_VDIM:DA_VDIM + 1, :]
    o = (o[:, 0:tq] - lam_ref[0] * o[:, tq:2 * tq]).T
    ms = jnp.mean(o * o, axis=-1, keepdims=True)
    o = o * lax.rsqrt(ms + EPS) * nw_ref[...] * lam_ref[1]
    o_ref[...] = (o * _silu(g_ref[...])).astype(BF16)


def _attention(lamv, q_t, k_r, v_t, proj, nw, bsz, seq, tq=1024, tk=512):
    m = k_r.shape[0]
    tq, tk = min(tq, seq), min(tk, seq)
    nq, nkb = seq // tq, seq // tk
    assert 2 * tq // ATTN_CW > ATTN_AHEAD
    return pl.pallas_call(
        functools.partial(_attn_kernel, tq=tq, tk=tk, nkb=nkb),
        grid=(bsz, DA_HEADS, nq),
        in_specs=[
            pl.BlockSpec(memory_space=pltpu.SMEM),
            pl.BlockSpec((LANE, tq), lambda b, h, i: (h, b * nq + i)),
            pl.BlockSpec((seq, LANE), lambda b, h, i: (b, h)),
            pl.BlockSpec((V_ROWS, seq), lambda b, h, i: (h, b)),
            pl.BlockSpec((tq, LANE), lambda b, h, i: (b * nq + i, C_GA + h)),
            pl.BlockSpec((1, LANE), lambda b, h, i: (0, 0)),
        ],
        out_specs=pl.BlockSpec((tq, LANE), lambda b, h, i: (b * nq + i, h)),
        out_shape=jax.ShapeDtypeStruct((m, DA_WIDTH), BF16),
        scratch_shapes=[
            pltpu.VMEM((LANE, 2 * tq), BF16),
            pltpu.VMEM((1, 2 * tq), F32),
            pltpu.VMEM((V_ROWS, 2 * tq), F32),
            pltpu.VMEM((2 * tq // ATTN_CW, tk, ATTN_CW), F32),
        ],
        compiler_params=_cparams(("parallel", "parallel", "arbitrary")),
        name="diff_attn",
    )(lamv, q_t, k_r, v_t, proj, nw)


def _hgrn_consts(rev):
    t = HG_T
    r = jnp.arange(t)[:, None]
    c = jnp.arange(t)[None, :]
    tri = (c >= r) if rev else (c <= r)
    masks = []
    for hh in HG_LEVELS:
        same = (r // (2 * hh)) == (c // (2 * hh))
        r_hi = (r % (2 * hh)) >= hh
        c_hi = (c % (2 * hh)) >= hh
        mk = same & ((~r_hi) & c_hi if rev else r_hi & (~c_hi))
        masks.append(mk)
    return tri.astype(BF16), jnp.stack(masks).astype(F32), jnp.ones((LANE, LANE), BF16)


def _hgrn_kernel(*refs, rev):
    if rev:
        (q_ref, f_ref, v_ref, la_ref, lc_ref, tri_ref, msk_ref, ones_ref,
         of_ref, gb_ref, nw_ref, o_ref, st_scr) = refs
    else:
        q_ref, f_ref, v_ref, la_ref, lc_ref, tri_ref, msk_ref, ones_ref, o_ref, st_scr = refs
    t = HG_T

    @pl.when(pl.program_id(2) == 0)
    def _():
        st_scr[...] = jnp.zeros(st_scr.shape, F32)

    q = _silu(q_ref[...])
    fr = f_ref[...]
    v = v_ref[...]
    y = lc_ref[...] + (jnp.minimum(fr, 0.0) - _log1pexp_negabs(fr))
    la = la_ref[...]
    g = jnp.maximum(la, y) + _log1pexp_negabs(la - y)
    f = jnp.exp(g)
    k = 1.0 - f
    cum = _sel_dot(tri_ref[...], g)
    tot = cum[0:1, :] if rev else cum[t - 1:t, :]
    qb, kb, vb = q.astype(BF16), k.astype(BF16), v.astype(BF16)

    o = _dot_nt((q * jnp.exp(cum)).astype(BF16), st_scr[...].astype(BF16))

    amat = jnp.zeros((t, t), F32)
    for li, hh in enumerate(HG_LEVELS):
        nb = t // (2 * hh)
        cum3 = cum.reshape(nb, 2 * hh, LANE)
        idx = hh if rev else hh - 1
        ref = jnp.broadcast_to(cum3[:, idx:idx + 1, :], cum3.shape).reshape(t, LANE)
        qt = q * jnp.exp(jnp.minimum(cum - ref, 0.0))
        kt = k * jnp.exp(jnp.minimum(ref - cum, 0.0))
        amat = amat + msk_ref[li] * _dot_nt(qt.astype(BF16), kt.astype(BF16))
    o = o + _dot(amat.astype(BF16), vb)

    row8 = lax.broadcasted_iota(jnp.int32, (t, LANE), 0) & (SUBLANE - 1)
    ones = ones_ref[...]
    o = o + _dot((q * k).astype(BF16), ones) * v
    prod = None
    for d in range(1, SUBLANE):
        sh = (t - d) if rev else d
        if d == 1:
            prod = f
        else:
            prod = prod * pltpu.roll(f, (t - (d - 1)) if rev else (d - 1), 0)
        valid = (row8 + d <= SUBLANE - 1) if rev else (row8 >= d)
        term = jnp.where(valid, q * pltpu.roll(k, sh, 0) * prod, 0.0)
        o = o + _dot(term.astype(BF16), ones) * pltpu.roll(v, sh, 0)

    kh = (k * jnp.exp(tot - cum)).astype(BF16)
    st_scr[...] = st_scr[...] * jnp.exp(tot) + _dot(v.T.astype(BF16), kh)

    if rev:
        o = o + of_ref[...]
        ms = jnp.mean(o * o, axis=-1, keepdims=True)
        o = o * lax.rsqrt(ms + EPS) * nw_ref[...]
        o_ref[...] = (o * _silu(gb_ref[...])).astype(BF16)
    else:
        o_ref[...] = o


def _hgrn(proj, log_lb, log1m_lb, nw, bsz, seq):
    m = proj.shape[0]
    t = HG_T
    nc = seq // t
    outs = None
    for rev in (False, True):
        tri, msk, ones = _hgrn_consts(rev)
        if rev:
            rowmap = lambda b, h, c: b * nc + (nc - 1 - c)
        else:
            rowmap = lambda b, h, c: b * nc + c

        def col(c0, rowmap=rowmap):
            return pl.BlockSpec((t, LANE), lambda b, h, c: (rowmap(b, h, c), c0 + h))

        lane_row = pl.BlockSpec((1, LANE), lambda b, h, c: (0, h))
        const2 = lambda shp: pl.BlockSpec(shp, lambda b, h, c: (0,) * len(shp))
        in_specs = [col(C_QB), col(C_FB if rev else C_FF), col(C_IB), lane_row, lane_row,
                    const2((t, t)), const2((len(HG_LEVELS), t, t)), const2((LANE, LANE))]
        args = [proj, proj, proj, log_lb, log1m_lb, tri, msk, ones]
        if rev:
            in_specs += [col(0), col(C_GB), pl.BlockSpec((1, LANE), lambda b, h, c: (0, 0))]
            args += [outs, proj, nw]
        outs = pl.pallas_call(
            functools.partial(_hgrn_kernel, rev=rev),
            grid=(bsz, HG_HEADS, nc),
            in_specs=in_specs,
            out_specs=col(0),
            out_shape=jax.ShapeDtypeStruct((m, HG_WIDTH), BF16 if rev else F32),
            scratch_shapes=[pltpu.VMEM((HG_DIM, HG_DIM), F32)],
            compiler_params=_cparams(("parallel", "parallel", "arbitrary")),
            name="hgrn_bwd" if rev else "hgrn_fwd",
        )(*args)
    return outs


def _conv_kernel(x_ref, p_ref, n_ref, w_ref, b_ref, o_ref, ext_scr, *, tl, nb):
    i = pl.program_id(0)
    first = (i % nb) == 0
    last = (i % nb) == nb - 1
    halo = SUBLANE
    ext_scr[0:halo, :] = jnp.where(first, 0.0, p_ref[...])
    ext_scr[halo:halo + tl, :] = x_ref[...]
    ext_scr[halo + tl:2 * halo + tl, :] = jnp.where(last, 0.0, n_ref[...])
    acc = jnp.broadcast_to(b_ref[...], (tl, x_ref.shape[1]))
    pad = SSM_CONV // 2
    for jj in range(SSM_CONV):
        acc = acc + w_ref[jj:jj + 1, :] * ext_scr[halo - pad + jj:halo - pad + jj + tl, :]
    o_ref[...] = _silu(acc)


def _conv(proj, w, b, seq, tl=512):
    m = proj.shape[0]
    tl = min(tl, seq)
    nb = seq // tl
    cw = 512
    c0 = C_XBC * LANE // cw
    r8 = tl // SUBLANE
    last8 = m // SUBLANE - 1
    return pl.pallas_call(
        functools.partial(_conv_kernel, tl=tl, nb=nb),
        grid=(m // tl, SSM_CONV_CH // cw),
        in_specs=[
            pl.BlockSpec((tl, cw), lambda i, c: (i, c0 + c)),
            pl.BlockSpec((SUBLANE, cw), lambda i, c: (jnp.maximum(i * r8 - 1, 0), c0 + c)),
            pl.BlockSpec((SUBLANE, cw), lambda i, c: (jnp.minimum((i + 1) * r8, last8), c0 + c)),
            pl.BlockSpec((SSM_CONV, cw), lambda i, c: (0, c)),
            pl.BlockSpec((1, cw), lambda i, c: (0, c)),
        ],
        out_specs=pl.BlockSpec((tl, cw), lambda i, c: (i, c)),
        out_shape=jax.ShapeDtypeStruct((m, SSM_CONV_CH), F32),
        scratch_shapes=[pltpu.VMEM((tl + 2 * SUBLANE, cw), F32)],
        compiler_params=_cparams(("parallel", "parallel")),
        name="ssm_conv",
    )(proj, proj, proj, w, b)


def _ssd_consts(rev):
    t = SSD_T
    r = jnp.arange(t)[:, None]
    c = jnp.arange(t)[None, :]
    tri = ((c >= r) if rev else (c <= r)).astype(BF16)
    d = 1 if rev else 0
    lane = jnp.arange(LANE)[:, None]
    colhead = (jnp.arange(SSM_HPG * SSM_HEADDIM) // SSM_HEADDIM)[None, :]
    xm = jnp.stack([(lane == d * SSM_HEADS + g * SSM_HPG + colhead) for g in range(SSM_GROUPS)]).astype(BF16)
    return tri, xm


def _ssd_kernel(*refs, rev):
    if rev:
        (xs_ref, bc_ref, dt_ref, dtb_ref, arow_ref, tri_ref, xm_ref,
         yf_ref, z0_ref, z1_ref, drow_ref, nw_ref, o_ref, st_scr) = refs
    else:
        xs_ref, bc_ref, dt_ref, dtb_ref, arow_ref, tri_ref, xm_ref, o_ref, st_scr = refs
    t = SSD_T
    d = 1 if rev else 0
    gw = SSM_HPG * SSM_HEADDIM

    @pl.when(pl.program_id(1) == 0)
    def _():
        st_scr[...] = jnp.zeros(st_scr.shape, F32)

    dt = _softplus(dt_ref[...] + dtb_ref[...])
    a = dt * arow_ref[...]
    cum = _sel_dot(tri_ref[...], a)
    cum_t = cum.T
    r_i = lax.broadcasted_iota(jnp.int32, (t, t), 0)
    c_i = lax.broadcasted_iota(jnp.int32, (t, t), 1)
    tmask = (c_i >= r_i) if rev else (c_i <= r_i)
    lane_lo = lax.broadcasted_iota(jnp.int32, (t, LANE), 1) < SSM_HEADDIM
    last = 0 if rev else t - 1

    for g in range(SSM_GROUPS):
        xm = xm_ref[g]
        e_cum = _dot_sel(cum, xm, 3)
        e_dt = _dot_sel(dt, xm, 2)
        e_tot = e_cum[last:last + 1, :]
        xs = xs_ref[:, g * gw:(g + 1) * gw]
        bm = bc_ref[:, g * LANE:(g + 1) * LANE]
        cm = bc_ref[:, (SSM_GROUPS + g) * LANE:(SSM_GROUPS + g + 1) * LANE]
        xd = xs * e_dt
        xdb = xd.astype(BF16)
        cmb = cm.astype(BF16)
        cb = _dot_nt(cmb, bm.astype(BF16))
        ys = []
        for p in range(SSM_HPG // 2):
            xp = xdb[:, p * LANE:(p + 1) * LANE]
            yy = []
            for jj in range(2):
                ln = d * SSM_HEADS + g * SSM_HPG + 2 * p + jj
                diff = cum[:, ln:ln + 1] - cum_t[ln:ln + 1, :]
                w = cb * jnp.exp(jnp.where(tmask, diff, NEG_BIG))
                yy.append(_dot(w.astype(BF16), xp))
            ys.append(jnp.where(lane_lo, yy[0], yy[1]))
        y = jnp.concatenate(ys, axis=1)
        y = y + _dot(cmb, st_scr[g].astype(BF16)) * jnp.exp(e_cum)
        xdec = (xd * jnp.exp(e_tot - e_cum)).astype(BF16)
        st_scr[g] = st_scr[g] * jnp.exp(e_tot) + _dot(bm.T.astype(BF16), xdec)

        if rev:
            sl = slice(g * gw, (g + 1) * gw)
            y = y + yf_ref[:, sl] + drow_ref[:, sl] * xs
            z = (z1_ref if g else z0_ref)[...]
            y = y * _silu(z)
            ms = jnp.mean(y * y, axis=-1, keepdims=True)
            o_ref[:, sl] = (y * lax.rsqrt(ms + EPS) * nw_ref[:, sl]).astype(BF16)
        else:
            o_ref[:, g * gw:(g + 1) * gw] = y


def _ssd(proj, xc, dtb, arow, drow, nw, bsz, seq):
    m = proj.shape[0]
    t = SSD_T
    nc = seq // t
    outs = None
    for rev in (False, True):
        tri, xm = _ssd_consts(rev)
        if rev:
            rowmap = lambda b, c: b * nc + (nc - 1 - c)
        else:
            rowmap = lambda b, c: b * nc + c

        def blk(width, cidx, rowmap=rowmap):
            return pl.BlockSpec((t, width), lambda b, c: (rowmap(b, c), cidx))

        const2 = lambda shp: pl.BlockSpec(shp, lambda b, c: (0,) * len(shp))
        in_specs = [blk(SSM_WIDTH, 0), blk(512, 2), blk(LANE, C_DT), const2((1, LANE)), const2((1, LANE)),
                    const2((t, t)), const2((SSM_GROUPS, LANE, 512))]
        args = [xc, xc, proj, dtb, arow, tri, xm]
        if rev:
            zc0 = C_ZC * LANE // 512
            in_specs += [blk(SSM_WIDTH, 0), blk(512, zc0), blk(512, zc0 + 1),
                         const2((1, SSM_WIDTH)), const2((1, SSM_WIDTH))]
            args += [outs, proj, proj, drow, nw]
        outs = pl.pallas_call(
            functools.partial(_ssd_kernel, rev=rev),
            grid=(bsz, nc),
            in_specs=in_specs,
            out_specs=blk(SSM_WIDTH, 0),
            out_shape=jax.ShapeDtypeStruct((m, SSM_WIDTH), BF16 if rev else F32),
            scratch_shapes=[pltpu.VMEM((SSM_GROUPS, SSM_STATE, 512), F32)],
            compiler_params=_cparams(("parallel", "arbitrary")),
            name="ssd_bwd" if rev else "ssd_fwd",
        )(*args)
    return outs


def _outproj_kernel(x_ref, a_ref, b_ref, c_ref, w_ref, o_ref):
    acc = _dot(a_ref[...], w_ref[0:DA_WIDTH, :])
    acc = acc + _dot(b_ref[...], w_ref[DA_WIDTH:DA_WIDTH + HG_WIDTH, :])
    acc = acc + _dot(c_ref[...], w_ref[DA_WIDTH + HG_WIDTH:MIX_WIDTH, :])
    o_ref[...] = x_ref[...] + acc


def _outproj(x2, oa, ob, oc, w_bf16, tm=512):
    m = x2.shape[0]
    tm = min(tm, m)
    return pl.pallas_call(
        _outproj_kernel,
        grid=(m // tm,),
        in_specs=[
            pl.BlockSpec((tm, D_MODEL), lambda i: (i, 0)),
            pl.BlockSpec((tm, DA_WIDTH), lambda i: (i, 0)),
            pl.BlockSpec((tm, HG_WIDTH), lambda i: (i, 0)),
            pl.BlockSpec((tm, SSM_WIDTH), lambda i: (i, 0)),
            pl.BlockSpec((MIX_WIDTH, D_MODEL), lambda i: (0, 0)),
        ],
        out_specs=pl.BlockSpec((tm, D_MODEL), lambda i: (i, 0)),
        out_shape=jax.ShapeDtypeStruct((m, D_MODEL), F32),
        compiler_params=_cparams(("parallel",)),
        name="outproj",
    )(x2, oa, ob, oc, w_bf16)


def _norm_kernel(x_ref, w_ref, o_ref):
    x = x_ref[...]
    ms = jnp.mean(x * x, axis=-1, keepdims=True)
    o_ref[...] = x * lax.rsqrt(ms + EPS) * w_ref[...]


def _final_norm(x2, w, tm=512):
    m = x2.shape[0]
    tm = min(tm, m)
    return pl.pallas_call(
        _norm_kernel,
        grid=(m // tm,),
        in_specs=[pl.BlockSpec((tm, D_MODEL), lambda i: (i, 0)), pl.BlockSpec((1, D_MODEL), lambda i: (0, 0))],
        out_specs=pl.BlockSpec((tm, D_MODEL), lambda i: (i, 0)),
        out_shape=jax.ShapeDtypeStruct((m, D_MODEL), F32),
        compiler_params=_cparams(("parallel",)),
        name="final_norm",
    )(x2, w)


def _layer_params(p, i, lb_all):
    lam_init = 0.8 - 0.6 * math.exp(-0.3 * i)
    dl = p["diff_lambda"][i].astype(F32)
    lam = jnp.exp(jnp.sum(dl[0] * dl[1])) - jnp.exp(jnp.sum(dl[2] * dl[3])) + lam_init
    lb = lb_all[i]
    pad32 = lambda v: jnp.pad(v.reshape(1, -1), ((0, 0), (0, LANE - 2 * SSM_HEADS)))
    return dict(
        norm_w=p["norm_w"][i].reshape(1, -1),
        w_in=p["w_in_bf16"][i],
        w_out=p["w_out_bf16"][i],
        lamv=jnp.stack([lam, jnp.asarray(1.0 - lam_init, F32)]).astype(F32),
        diff_nw=p["diff_norm_w"][i].reshape(1, -1),
        log_lb=jnp.log(lb).reshape(1, -1),
        log1m_lb=jnp.log1p(-lb).reshape(1, -1),
        hgrn_nw=p["hgrn_norm_w"][i].reshape(1, -1),
        conv_w=p["conv_w"][i],
        conv_b=p["conv_b"][i].reshape(1, -1),
        dtb=pad32(p["ssm_dt_bias"][i].astype(F32)),
        arow=pad32(-jnp.exp(p["ssm_A_log"][i].astype(F32))),
        drow=jnp.repeat(p["ssm_D"][i].astype(F32), SSM_HEADDIM).reshape(1, -1),
        ssm_nw=p["ssm_norm_w"][i].reshape(1, -1),
    )


def _trunk(x, p, layers):
    bsz, seq, _ = x.shape
    x2 = x.reshape(bsz * seq, D_MODEL)
    tabs = _rope_tables(seq)
    for lp in layers:
        proj = _inproj(x2, lp["norm_w"], lp["w_in"])
        q_t, k_r, v_t = _prep(proj, tabs, seq)
        oa = _attention(lp["lamv"], q_t, k_r, v_t, proj, lp["diff_nw"], bsz, seq)
        ob = _hgrn(proj, lp["log_lb"], lp["log1m_lb"], lp["hgrn_nw"], bsz, seq)
        xc = _conv(proj, lp["conv_w"], lp["conv_b"], seq)
        oc = _ssd(proj, xc, lp["dtb"], lp["arow"], lp["drow"], lp["ssm_nw"], bsz, seq)
        x2 = _outproj(x2, oa, ob, oc, lp["w_out"])
    y = _final_norm(x2, p["final_norm_w"].reshape(1, -1))
    return y.reshape(bsz, seq, D_MODEL)


def kernel(x_prompt, x_sample, norm_w, w_in, w_out, diff_lambda, diff_norm_w, hgrn_lower_bounds, hgrn_norm_w,
           conv_w, conv_b, ssm_A_log, ssm_dt_bias, ssm_D, ssm_norm_w, final_norm_w):
    p = dict(norm_w=norm_w, diff_lambda=diff_lambda, diff_norm_w=diff_norm_w, hgrn_norm_w=hgrn_norm_w,
             conv_w=conv_w, conv_b=conv_b, ssm_A_log=ssm_A_log, ssm_dt_bias=ssm_dt_bias, ssm_D=ssm_D,
             ssm_norm_w=ssm_norm_w, final_norm_w=final_norm_w)
    p["w_in_bf16"] = jnp.pad(w_in, ((0, 0), (0, 0), (0, PROJ_PAD - PROJ_WIDTH))).astype(BF16)
    p["w_out_bf16"] = w_out.astype(BF16)
    lb_all = jnp.cumsum(jax.nn.softmax(hgrn_lower_bounds.astype(F32), axis=0), axis=0)
    lb_all = lb_all - lb_all[0]
    layers = [_layer_params(p, i, lb_all) for i in range(DEPTH)]
    return (_trunk(x_prompt, p, layers), _trunk(x_sample, p, layers))
```

```python
import functools
import math

import jax
import jax.numpy as jnp
from jax import lax
from jax.experimental import pallas as pl
from jax.experimental.pallas import tpu as pltpu

F32 = jnp.float32
BF16 = jnp.bfloat16

D_MODEL = 2048
DEPTH = 4
DA_HEADS = 4
DA_DIM = 64
DA_VDIM = 128
DA_WIDTH = 512
ROPE_THETA = 500000.0
ROPE_DIMS = 16
HG_HEADS = 4
HG_DIM = 128
HG_WIDTH = 512
SSM_HEADS = 16
SSM_HEADDIM = 64
SSM_WIDTH = 1024
SSM_GROUPS = 2
SSM_HPG = 8
SSM_STATE = 128
SSM_CONV = 5
SSM_CONV_CH = 1536
MIX_WIDTH = 2048
PROJ_WIDTH = 7200
EPS = 1e-6

LANE = 128
SUBLANE = 8
PROJ_PAD = 7680
C_QA, C_KA, C_VA, C_GA = 0, 4, 8, 12
C_QB, C_FF, C_FB, C_IB, C_GB = 16, 20, 24, 28, 32
C_ZC, C_XBC, C_DT = 36, 44, 56

VMEM_LIMIT = 56 * 1024 * 1024

HG_T = 256
HG_TOP = (128, 64, 32, 16, 8)
HG_SUB = (4, 2, 1)
SSD_T = 128
NEG_BIG = -1e30
LOG2E = 1.4426950408889634
V_ROWS = 144
ATTN_CW = 256
ATTN_KBI = 8
ATTN_AHEAD = 3


def _cparams(sem):
    return pltpu.CompilerParams(dimension_semantics=sem, vmem_limit_bytes=VMEM_LIMIT)


def _sigmoid(x):
    return 1.0 / (1.0 + jnp.exp(-x))


def _silu(x):
    return x * _sigmoid(x)


def _log1pexp_negabs(x):
    return jnp.log1p(jnp.exp(-jnp.abs(x)))


def _softplus(x):
    return jnp.maximum(x, 0.0) + _log1pexp_negabs(x)


def _dot(a, b):
    return jnp.dot(a, b, preferred_element_type=F32)


def _dot_nt(a, b):
    return lax.dot_general(a, b, (((1,), (1,)), ((), ())), preferred_element_type=F32)


def _split(x, parts):
    out = []
    r = x
    for i in range(parts):
        p = r.astype(BF16)
        out.append(p)
        if i + 1 < parts:
            r = r - p.astype(F32)
    return out


def _sel_dot(sel, x, parts=3):
    ps = _split(x, parts)
    acc = _dot(sel, ps[-1])
    for p in ps[-2::-1]:
        acc = acc + _dot(sel, p)
    return acc


def _dot_sel(x, sel, parts=3):
    ps = _split(x, parts)
    acc = _dot(ps[-1], sel)
    for p in ps[-2::-1]:
        acc = acc + _dot(p, sel)
    return acc


def _inproj_kernel(x_ref, nw_ref, w_ref, o_ref, h_scr):
    @pl.when(pl.program_id(1) == 0)
    def _():
        x = x_ref[...]
        ms = jnp.mean(x * x, axis=-1, keepdims=True)
        h_scr[...] = (x * lax.rsqrt(ms + EPS) * nw_ref[...]).astype(BF16)

    o_ref[...] = _dot(h_scr[...], w_ref[...])


def _inproj(x2, nw, w_bf16, tm=1024, tn=1536):
    m = x2.shape[0]
    tm = min(tm, m)
    return pl.pallas_call(
        _inproj_kernel,
        grid=(m // tm, PROJ_PAD // tn),
        in_specs=[
            pl.BlockSpec((tm, D_MODEL), lambda i, j: (i, 0)),
            pl.BlockSpec((1, D_MODEL), lambda i, j: (0, 0)),
            pl.BlockSpec((D_MODEL, tn), lambda i, j: (0, j)),
        ],
        out_specs=pl.BlockSpec((tm, tn), lambda i, j: (i, j)),
        out_shape=jax.ShapeDtypeStruct((m, PROJ_PAD), F32),
        scratch_shapes=[pltpu.VMEM((tm, D_MODEL), BF16)],
        compiler_params=_cparams(("parallel", "arbitrary")),
        name="inproj",
    )(x2, nw, w_bf16)


def _rope_tables(seq):
    half = ROPE_DIMS // 2
    inv = ROPE_THETA ** (-jnp.arange(0, ROPE_DIMS, 2, dtype=F32) / ROPE_DIMS)
    ang = jnp.arange(seq, dtype=F32)[:, None] * inv[None, :]
    cos, sin = jnp.cos(ang), jnp.sin(ang)
    ones = jnp.ones((seq, DA_DIM - ROPE_DIMS), F32)
    zeros = jnp.zeros((seq, DA_DIM - ROPE_DIMS), F32)
    zh = jnp.zeros((seq, half), F32)
    c64 = jnp.concatenate([cos, cos, ones], axis=1)
    s1_64 = jnp.concatenate([-sin, zh, zeros], axis=1)
    s2_64 = jnp.concatenate([zh, sin, zeros], axis=1)
    return (jnp.concatenate([c64, c64], axis=1), jnp.concatenate([s1_64, s1_64], axis=1),
            jnp.concatenate([s2_64, s2_64], axis=1))


def _prep_kernel(q_ref, k_ref, v_ref, c_ref, s1_ref, s2_ref, qo_ref, ko_ref, vo_ref):
    c, s1, s2 = c_ref[...], s1_ref[...], s2_ref[...]
    qmul = DA_DIM ** -0.5 * LOG2E
    tr = q_ref.shape[0]
    for h in range(DA_HEADS):
        sl = slice(h * LANE, (h + 1) * LANE)
        for src, is_q in ((q_ref, True), (k_ref, False)):
            t = src[:, sl]
            r = t * c + pltpu.roll(t, LANE - ROPE_DIMS // 2, 1) * s1 + pltpu.roll(t, ROPE_DIMS // 2, 1) * s2
            if is_q:
                qo_ref[sl, :] = (r * qmul).T.astype(BF16)
            else:
                ko_ref[:, sl] = r.astype(BF16)
        vo_ref[h * V_ROWS:h * V_ROWS + DA_VDIM, :] = v_ref[:, sl].T.astype(BF16)
        vo_ref[h * V_ROWS + DA_VDIM:(h + 1) * V_ROWS, :] = jnp.ones((V_ROWS - DA_VDIM, tr), BF16)


def _prep(proj, tabs, seq, tr=512):
    m = proj.shape[0]
    tr = min(tr, seq)
    nb = seq // tr
    w = DA_WIDTH
    tab_spec = pl.BlockSpec((tr, LANE), lambda i: (i % nb, 0))
    return pl.pallas_call(
        _prep_kernel,
        grid=(m // tr,),
        in_specs=[
            pl.BlockSpec((tr, w), lambda i: (i, C_QA * LANE // w)),
            pl.BlockSpec((tr, w), lambda i: (i, C_KA * LANE // w)),
            pl.BlockSpec((tr, w), lambda i: (i, C_VA * LANE // w)),
            tab_spec, tab_spec, tab_spec,
        ],
        out_specs=[pl.BlockSpec((w, tr), lambda i: (0, i)),
                   pl.BlockSpec((tr, w), lambda i: (i, 0)),
                   pl.BlockSpec((DA_HEADS * V_ROWS, tr), lambda i: (0, i))],
        out_shape=[jax.ShapeDtypeStruct((w, m), BF16), jax.ShapeDtypeStruct((m, w), BF16),
                   jax.ShapeDtypeStruct((DA_HEADS * V_ROWS, m), BF16)],
        compiler_params=_cparams(("parallel",)),
        name="rope_prep",
    )(proj, proj, proj, *tabs)


def _attn_kernel(lam_ref, q_ref, k_ref, v_ref, g_ref, nw_ref, o_ref, qs_scr, m_scr, acc_scr, s_scr, *, tq, tk, nkb):
    q = q_ref[...]
    row = lax.broadcasted_iota(jnp.int32, q.shape, 0)
    zero = jnp.zeros_like(q)
    qs_scr[:, 0:tq] = jnp.where(row < DA_DIM, q, zero)
    qs_scr[:, tq:2 * tq] = jnp.where(row >= DA_DIM, q, zero)
    m_scr[...] = jnp.full(m_scr.shape, -jnp.inf, F32)
    acc_scr[...] = jnp.zeros(acc_scr.shape, F32)
    ncc = 2 * tq // ATTN_CW

    def scores(kb, c):
        kblk = k_ref[pl.ds(pl.multiple_of(kb * tk, tk), tk), :]
        return _dot(kblk, qs_scr[:, c * ATTN_CW:(c + 1) * ATTN_CW])

    for c in range(ATTN_AHEAD):
        s_scr[c] = scores(0, c)

    def body(it, carry):
        for kk in range(kbi):
            kb = it * kbi + kk
            vt = v_ref[:, pl.ds(pl.multiple_of(kb * tk, tk), tk)]
            kb_next = jnp.minimum(kb + 1, nkb - 1)
            for c in range(ncc):
                cs = slice(c * ATTN_CW, (c + 1) * ATTN_CW)
                ca = c + ATTN_AHEAD
                s_scr[ca % ncc] = scores(kb, ca) if ca < ncc else scores(kb_next, ca - ncc)
                st = s_scr[c]
                m_prev = m_scr[:, cs]
                m_new = jnp.maximum(m_prev, jnp.max(st, axis=0, keepdims=True))
                alpha = jnp.exp2(m_prev - m_new)
                pt = jnp.exp2(st - m_new).astype(BF16)
                acc_scr[:, cs] = acc_scr[:, cs] * alpha + _dot(vt, pt)
                m_scr[:, cs] = m_new
        return carry

    kbi = math.gcd(ATTN_KBI, nkb)
    lax.fori_loop(0, nkb // kbi, body, 0)

    a = acc_scr[...]
    o = a[0:DA_VDIM, :] / a[DA_VDIM:DA_VDIM + 1, :]
    o = (o[:, 0:tq] - lam_ref[0] * o[:, tq:2 * tq]).T
    ms = jnp.mean(o * o, axis=-1, keepdims=True)
    o = o * lax.rsqrt(ms + EPS) * nw_ref[...] * lam_ref[1]
    o_ref[...] = (o * _silu(g_ref[...])).astype(BF16)


def _attention(lamv, q_t, k_r, v_t, proj, nw, bsz, seq, tq=1024, tk=512):
    m = k_r.shape[0]
    tq, tk = min(tq, seq), min(tk, seq)
    nq, nkb = seq // tq, seq // tk
    assert 2 * tq // ATTN_CW > ATTN_AHEAD
    return pl.pallas_call(
        functools.partial(_attn_kernel, tq=tq, tk=tk, nkb=nkb),
        grid=(bsz, DA_HEADS, nq),
        in_specs=[
            pl.BlockSpec(memory_space=pltpu.SMEM),
            pl.BlockSpec((LANE, tq), lambda b, h, i: (h, b * nq + i)),
            pl.BlockSpec((seq, LANE), lambda b, h, i: (b, h)),
            pl.BlockSpec((V_ROWS, seq), lambda b, h, i: (h, b)),
            pl.BlockSpec((tq, LANE), lambda b, h, i: (b * nq + i, C_GA + h)),
            pl.BlockSpec((1, LANE), lambda b, h, i: (0, 0)),
        ],
        out_specs=pl.BlockSpec((tq, LANE), lambda b, h, i: (b * nq + i, h)),
        out_shape=jax.ShapeDtypeStruct((m, DA_WIDTH), BF16),
        scratch_shapes=[
            pltpu.VMEM((LANE, 2 * tq), BF16),
            pltpu.VMEM((1, 2 * tq), F32),
            pltpu.VMEM((V_ROWS, 2 * tq), F32),
            pltpu.VMEM((2 * tq // ATTN_CW, tk, ATTN_CW), F32),
        ],
        compiler_params=_cparams(("parallel", "parallel", "arbitrary")),
        name="diff_attn",
    )(lamv, q_t, k_r, v_t, proj, nw)


def _hgrn_consts(rev):
    t = HG_T
    r = jnp.arange(t)[:, None]
    c = jnp.arange(t)[None, :]
    tri = (c >= r) if rev else (c <= r)
    half = t // 2
    rr = jnp.arange(half)[:, None]
    cc = jnp.arange(half)[None, :]
    masks = [(rr // hh) == (cc // hh) for hh in HG_TOP]
    for hh in HG_SUB:
        same = (rr // (2 * hh)) == (cc // (2 * hh))
        r_hi = (rr % (2 * hh)) >= hh
        c_hi = (cc % (2 * hh)) >= hh
        masks.append(same & ((~r_hi) & c_hi if rev else r_hi & (~c_hi)))
    masks.append(rr == cc)
    return tri.astype(BF16), jnp.stack(masks).astype(F32)


def _hgrn_kernel(*refs, rev):
    if rev:
        (q_ref, f_ref, v_ref, la_ref, lc_ref, tri_ref, msk_ref,
         of_ref, gb_ref, nw_ref, o_ref, st_scr) = refs
    else:
        q_ref, f_ref, v_ref, la_ref, lc_ref, tri_ref, msk_ref, o_ref, st_scr = refs
    t = HG_T
    half = t // 2
    heads = range(HG_HEADS)
    hs = [slice(h * LANE, (h + 1) * LANE) for h in heads]

    @pl.when(pl.program_id(1) == 0)
    def _():
        st_scr[...] = jnp.zeros(st_scr.shape, F32)

    q = [_silu(q_ref[:, hs[h]]) for h in heads]
    v = [v_ref[:, hs[h]] for h in heads]
    k, cum = [], []
    tri = tri_ref[...]
    for h in heads:
        fr = f_ref[:, hs[h]]
        y = lc_ref[:, hs[h]] + (jnp.minimum(fr, 0.0) - _log1pexp_negabs(fr))
        la = la_ref[:, hs[h]]
        g = jnp.maximum(la, y) + _log1pexp_negabs(la - y)
        k.append(1.0 - jnp.exp(g))
        cum.append(_sel_dot(tri, g * LOG2E))
    tot = [c[0:1, :] if rev else c[t - 1:t, :] for c in cum]

    o = [_dot_nt((q[h] * jnp.exp2(cum[h])).astype(BF16), st_scr[h].astype(BF16)) for h in heads]

    for li, hh in enumerate(HG_TOP):
        nb = t // (2 * hh)

        def halves(x, nb=nb, hh=hh):
            x4 = x.reshape(nb, 2, hh, LANE)
            return x4[:, 0], x4[:, 1]

        for h in heads:
            c_lo, c_hi = halves(cum[h])
            q_lo, q_hi = halves(q[h])
            k_lo, k_hi = halves(k[h])
            v_lo, v_hi = halves(v[h])
            if rev:
                ref = c_hi[:, 0:1, :]
                qe, ke, vv = q_lo * jnp.exp2(c_lo - ref), k_hi * jnp.exp2(ref - c_hi), v_hi
            else:
                ref = c_lo[:, hh - 1:hh, :]
                qe, ke, vv = q_hi * jnp.exp2(c_hi - ref), k_lo * jnp.exp2(ref - c_lo), v_lo
            a = _dot_nt(qe.reshape(half, LANE).astype(BF16), ke.reshape(half, LANE).astype(BF16))
            if nb > 1:
                a = a * msk_ref[li]
            part = _dot(a.astype(BF16), vv.reshape(half, LANE).astype(BF16)).reshape(nb, hh, LANE)
            zero = jnp.zeros_like(part)
            o[h] = o[h] + jnp.stack([part, zero] if rev else [zero, part], axis=1).reshape(t, LANE)

    row8 = lax.broadcasted_iota(jnp.int32, (t, LANE), 0) & (SUBLANE - 1)
    nt = len(HG_TOP)
    qb = [x.astype(BF16) for x in q]
    kb = [x.astype(BF16) for x in k]
    vb = [x.astype(BF16) for x in v]
    asub = [[msk_ref[nt + len(HG_SUB)] * _dot_nt(qb[h][i * half:(i + 1) * half], kb[h][i * half:(i + 1) * half])
             for i in range(2)] for h in heads]
    for si, hh in enumerate(HG_SUB):
        rows = [b * 2 * hh + (hh if rev else hh - 1) for b in range(SUBLANE // (2 * hh))]
        for h in heads:
            cum3 = cum[h].reshape(t // SUBLANE, SUBLANE, LANE)

            def bcast_row(r, cum3=cum3):
                return jnp.broadcast_to(cum3[:, r:r + 1, :], cum3.shape).reshape(t, LANE)

            ref = bcast_row(rows[-1])
            for i in reversed(range(len(rows) - 1)):
                ref = jnp.where(row8 < (i + 1) * 2 * hh, bcast_row(rows[i]), ref)
            qt = (q[h] * jnp.exp2(jnp.minimum(cum[h] - ref, 0.0))).astype(BF16)
            kt = (k[h] * jnp.exp2(jnp.minimum(ref - cum[h], 0.0))).astype(BF16)
            for i in range(2):
                sl = slice(i * half, (i + 1) * half)
                asub[h][i] = asub[h][i] + msk_ref[nt + si] * _dot_nt(qt[sl], kt[sl])
    for h in heads:
        o[h] = o[h] + jnp.concatenate(
            [_dot(asub[h][i].astype(BF16), vb[h][i * half:(i + 1) * half]) for i in range(2)], axis=0)

    for h in heads:
        kh = (k[h] * jnp.exp2(tot[h] - cum[h])).astype(BF16)
        st_scr[h] = st_scr[h] * jnp.exp2(tot[h]) + _dot(v[h].T.astype(BF16), kh)

    for h in heads:
        if rev:
            oh = o[h] + of_ref[:, hs[h]]
            ms = jnp.mean(oh * oh, axis=-1, keepdims=True)
            oh = oh * lax.rsqrt(ms + EPS) * nw_ref[...]
            o_ref[:, hs[h]] = (oh * _silu(gb_ref[:, hs[h]])).astype(BF16)
        else:
            o_ref[:, hs[h]] = o[h]


def _hgrn(proj, log_lb, log1m_lb, nw, bsz, seq):
    m = proj.shape[0]
    t = HG_T
    w = HG_WIDTH
    nc = seq // t
    outs = None
    for rev in (False, True):
        tri, msk = _hgrn_consts(rev)
        if rev:
            rowmap = lambda b, c: b * nc + (nc - 1 - c)
        else:
            rowmap = lambda b, c: b * nc + c

        def col(c0, rowmap=rowmap):
            return pl.BlockSpec((t, w), lambda b, c: (rowmap(b, c), c0 * LANE // w))

        const2 = lambda shp: pl.BlockSpec(shp, lambda b, c: (0,) * len(shp))
        in_specs = [col(C_QB), col(C_FB if rev else C_FF), col(C_IB), const2((1, w)), const2((1, w)),
                    const2((t, t)), const2((len(HG_TOP) + len(HG_SUB) + 1, t // 2, t // 2))]
        args = [proj, proj, proj, log_lb, log1m_lb, tri, msk]
        if rev:
            in_specs += [col(0), col(C_GB), const2((1, LANE))]
            args += [outs, proj, nw]
        outs = pl.pallas_call(
            functools.partial(_hgrn_kernel, rev=rev),
            grid=(bsz, nc),
            in_specs=in_specs,
            out_specs=col(0),
            out_shape=jax.ShapeDtypeStruct((m, w), BF16 if rev else F32),
            scratch_shapes=[pltpu.VMEM((HG_HEADS, HG_DIM, HG_DIM), F32)],
            compiler_params=_cparams(("parallel", "arbitrary")),
            name="hgrn_bwd" if rev else "hgrn_fwd",
        )(*args)
    return outs


def _conv_kernel(x_ref, p_ref, n_ref, w_ref, b_ref, o_ref, ext_scr, *, tl, nb):
    i = pl.program_id(0)
    first = (i % nb) == 0
    last = (i % nb) == nb - 1
    halo = SUBLANE
    ext_scr[0:halo, :] = jnp.where(first, 0.0, p_ref[...])
    ext_scr[halo:halo + tl, :] = x_ref[...]
    ext_scr[halo + tl:2 * halo + tl, :] = jnp.where(last, 0.0, n_ref[...])
    acc = jnp.broadcast_to(b_ref[...], (tl, x_ref.shape[1]))
    pad = SSM_CONV // 2
    for jj in range(SSM_CONV):
        acc = acc + w_ref[jj:jj + 1, :] * ext_scr[halo - pad + jj:halo - pad + jj + tl, :]
    o_ref[...] = _silu(acc)


def _conv(proj, w, b, seq, tl=512):
    m = proj.shape[0]
    tl = min(tl, seq)
    nb = seq // tl
    cw = 512
    c0 = C_XBC * LANE // cw
    r8 = tl // SUBLANE
    last8 = m // SUBLANE - 1
    return pl.pallas_call(
        functools.partial(_conv_kernel, tl=tl, nb=nb),
        grid=(m // tl, SSM_CONV_CH // cw),
        in_specs=[
            pl.BlockSpec((tl, cw), lambda i, c: (i, c0 + c)),
            pl.BlockSpec((SUBLANE, cw), lambda i, c: (jnp.maximum(i * r8 - 1, 0), c0 + c)),
            pl.BlockSpec((SUBLANE, cw), lambda i, c: (jnp.minimum((i + 1) * r8, last8), c0 + c)),
            pl.BlockSpec((SSM_CONV, cw), lambda i, c: (0, c)),
            pl.BlockSpec((1, cw), lambda i, c: (0, c)),
        ],
        out_specs=pl.BlockSpec((tl, cw), lambda i, c: (i, c)),
        out_shape=jax.ShapeDtypeStruct((m, SSM_CONV_CH), F32),
        scratch_shapes=[pltpu.VMEM((tl + 2 * SUBLANE, cw), F32)],
        compiler_params=_cparams(("parallel", "parallel")),
        name="ssm_conv",
    )(proj, proj, proj, w, b)


def _ssd_consts(rev):
    t = SSD_T
    r = jnp.arange(t)[:, None]
    c = jnp.arange(t)[None, :]
    tri = ((c >= r) if rev else (c <= r)).astype(BF16)
    d = 1 if rev else 0
    lane = jnp.arange(LANE)[:, None]
    colhead = (jnp.arange(SSM_HPG * SSM_HEADDIM) // SSM_HEADDIM)[None, :]
    xm = jnp.stack([(lane == d * SSM_HEADS + g * SSM_HPG + colhead) for g in range(SSM_GROUPS)]).astype(BF16)
    return tri, xm


def _ssd_kernel(*refs, rev):
    if rev:
        (xs_ref, bc_ref, dt_ref, dtb_ref, arow_ref, tri_ref, xm_ref,
         yf_ref, z0_ref, z1_ref, drow_ref, nw_ref, o_ref, st_scr) = refs
    else:
        xs_ref, bc_ref, dt_ref, dtb_ref, arow_ref, tri_ref, xm_ref, o_ref, st_scr = refs
    t = SSD_T
    d = 1 if rev else 0
    gw = SSM_HPG * SSM_HEADDIM

    @pl.when(pl.program_id(1) == 0)
    def _():
        st_scr[...] = jnp.zeros(st_scr.shape, F32)

    dt = _softplus(dt_ref[...] + dtb_ref[...])
    a = dt * arow_ref[...]
    cum = _sel_dot(tri_ref[...], a)
    cum_t = cum.T
    r_i = lax.broadcasted_iota(jnp.int32, (t, t), 0)
    c_i = lax.broadcasted_iota(jnp.int32, (t, t), 1)
    tmask = (c_i >= r_i) if rev else (c_i <= r_i)
    lane_lo = lax.broadcasted_iota(jnp.int32, (t, LANE), 1) < SSM_HEADDIM
    last = 0 if rev else t - 1

    for g in range(SSM_GROUPS):
        xm = xm_ref[g]
        e_cum = _dot_sel(cum, xm, 3)
        e_dt = _dot_sel(dt, xm, 2)
        e_tot = e_cum[last:last + 1, :]
        xs = xs_ref[:, g * gw:(g + 1) * gw]
        bm = bc_ref[:, g * LANE:(g + 1) * LANE]
        cm = bc_ref[:, (SSM_GROUPS + g) * LANE:(SSM_GROUPS + g + 1) * LANE]
        xd = xs * e_dt
        xdb = xd.astype(BF16)
        cmb = cm.astype(BF16)
        cb = _dot_nt(cmb, bm.astype(BF16))
        ys = []
        for p in range(SSM_HPG // 2):
            xp = xdb[:, p * LANE:(p + 1) * LANE]
            yy = []
            for jj in range(2):
                ln = d * SSM_HEADS + g * SSM_HPG + 2 * p + jj
                diff = cum[:, ln:ln + 1] - cum_t[ln:ln + 1, :]
                w = cb * jnp.exp(jnp.where(tmask, diff, NEG_BIG))
                yy.append(_dot(w.astype(BF16), xp))
            ys.append(jnp.where(lane_lo, yy[0], yy[1]))
        y = jnp.concatenate(ys, axis=1)
        y = y + _dot(cmb, st_scr[g].astype(BF16)) * jnp.exp(e_cum)
        xdec = (xd * jnp.exp(e_tot - e_cum)).astype(BF16)
        st_scr[g] = st_scr[g] * jnp.exp(e_tot) + _dot(bm.T.astype(BF16), xdec)

        if rev:
            sl = slice(g * gw, (g + 1) * gw)
            y = y + yf_ref[:, sl] + drow_ref[:, sl] * xs
            z = (z1_ref if g else z0_ref)[...]
            y = y * _silu(z)
            ms = jnp.mean(y * y, axis=-1, keepdims=True)
            o_ref[:, sl] = (y * lax.rsqrt(ms + EPS) * nw_ref[:, sl]).astype(BF16)
        else:
            o_ref[:, g * gw:(g + 1) * gw] = y


def _ssd(proj, xc, dtb, arow, drow, nw, bsz, seq):
    m = proj.shape[0]
    t = SSD_T
    nc = seq // t
    outs = None
    for rev in (False, True):
        tri, xm = _ssd_consts(rev)
        if rev:
            rowmap = lambda b, c: b * nc + (nc - 1 - c)
        else:
            rowmap = lambda b, c: b * nc + c

        def blk(width, cidx, rowmap=rowmap):
            return pl.BlockSpec((t, width), lambda b, c: (rowmap(b, c), cidx))

        const2 = lambda shp: pl.BlockSpec(shp, lambda b, c: (0,) * len(shp))
        in_specs = [blk(SSM_WIDTH, 0), blk(512, 2), blk(LANE, C_DT), const2((1, LANE)), const2((1, LANE)),
                    const2((t, t)), const2((SSM_GROUPS, LANE, 512))]
        args = [xc, xc, proj, dtb, arow, tri, xm]
        if rev:
            zc0 = C_ZC * LANE // 512
            in_specs += [blk(SSM_WIDTH, 0), blk(512, zc0), blk(512, zc0 + 1),
                         const2((1, SSM_WIDTH)), const2((1, SSM_WIDTH))]
            args += [outs, proj, proj, drow, nw]
        outs = pl.pallas_call(
            functools.partial(_ssd_kernel, rev=rev),
            grid=(bsz, nc),
            in_specs=in_specs,
            out_specs=blk(SSM_WIDTH, 0),
            out_shape=jax.ShapeDtypeStruct((m, SSM_WIDTH), BF16 if rev else F32),
            scratch_shapes=[pltpu.VMEM((SSM_GROUPS, SSM_STATE, 512), F32)],
            compiler_params=_cparams(("parallel", "arbitrary")),
            name="ssd_bwd" if rev else "ssd_fwd",
        )(*args)
    return outs


def _outproj_kernel(x_ref, a_ref, b_ref, c_ref, w_ref, o_ref):
    acc = _dot(a_ref[...], w_ref[0:DA_WIDTH, :])
    acc = acc + _dot(b_ref[...], w_ref[DA_WIDTH:DA_WIDTH + HG_WIDTH, :])
    acc = acc + _dot(c_ref[...], w_ref[DA_WIDTH + HG_WIDTH:MIX_WIDTH, :])
    o_ref[...] = x_ref[...] + acc


def _outproj(x2, oa, ob, oc, w_bf16, tm=512):
    m = x2.shape[0]
    tm = min(tm, m)
    return pl.pallas_call(
        _outproj_kernel,
        grid=(m // tm,),
        in_specs=[
            pl.BlockSpec((tm, D_MODEL), lambda i: (i, 0)),
            pl.BlockSpec((tm, DA_WIDTH), lambda i: (i, 0)),
            pl.BlockSpec((tm, HG_WIDTH), lambda i: (i, 0)),
            pl.BlockSpec((tm, SSM_WIDTH), lambda i: (i, 0)),
            pl.BlockSpec((MIX_WIDTH, D_MODEL), lambda i: (0, 0)),
        ],
        out_specs=pl.BlockSpec((tm, D_MODEL), lambda i: (i, 0)),
        out_shape=jax.ShapeDtypeStruct((m, D_MODEL), F32),
        compiler_params=_cparams(("parallel",)),
        name="outproj",
    )(x2, oa, ob, oc, w_bf16)


def _norm_kernel(x_ref, w_ref, o_ref):
    x = x_ref[...]
    ms = jnp.mean(x * x, axis=-1, keepdims=True)
    o_ref[...] = x * lax.rsqrt(ms + EPS) * w_ref[...]


def _final_norm(x2, w, tm=512):
    m = x2.shape[0]
    tm = min(tm, m)
    return pl.pallas_call(
        _norm_kernel,
        grid=(m // tm,),
        in_specs=[pl.BlockSpec((tm, D_MODEL), lambda i: (i, 0)), pl.BlockSpec((1, D_MODEL), lambda i: (0, 0))],
        out_specs=pl.BlockSpec((tm, D_MODEL), lambda i: (i, 0)),
        out_shape=jax.ShapeDtypeStruct((m, D_MODEL), F32),
        compiler_params=_cparams(("parallel",)),
        name="final_norm",
    )(x2, w)


def _layer_params(p, i, lb_all):
    lam_init = 0.8 - 0.6 * math.exp(-0.3 * i)
    dl = p["diff_lambda"][i].astype(F32)
    lam = jnp.exp(jnp.sum(dl[0] * dl[1])) - jnp.exp(jnp.sum(dl[2] * dl[3])) + lam_init
    lb = lb_all[i]
    pad32 = lambda v: jnp.pad(v.reshape(1, -1), ((0, 0), (0, LANE - 2 * SSM_HEADS)))
    return dict(
        norm_w=p["norm_w"][i].reshape(1, -1),
        w_in=p["w_in_bf16"][i],
        w_out=p["w_out_bf16"][i],
        lamv=jnp.stack([lam, jnp.asarray(1.0 - lam_init, F32)]).astype(F32),
        diff_nw=p["diff_norm_w"][i].reshape(1, -1),
        log_lb=jnp.log(lb).reshape(1, -1),
        log1m_lb=jnp.log1p(-lb).reshape(1, -1),
        hgrn_nw=p["hgrn_norm_w"][i].reshape(1, -1),
        conv_w=p["conv_w"][i],
        conv_b=p["conv_b"][i].reshape(1, -1),
        dtb=pad32(p["ssm_dt_bias"][i].astype(F32)),
        arow=pad32(-jnp.exp(p["ssm_A_log"][i].astype(F32))),
        drow=jnp.repeat(p["ssm_D"][i].astype(F32), SSM_HEADDIM).reshape(1, -1),
        ssm_nw=p["ssm_norm_w"][i].reshape(1, -1),
    )


def _trunk(x, p, layers):
    bsz, seq, _ = x.shape
    x2 = x.reshape(bsz * seq, D_MODEL)
    tabs = _rope_tables(seq)
    for lp in layers:
        proj = _inproj(x2, lp["norm_w"], lp["w_in"])
        q_t, k_r, v_t = _prep(proj, tabs, seq)
        oa = _attention(lp["lamv"], q_t, k_r, v_t, proj, lp["diff_nw"], bsz, seq)
        ob = _hgrn(proj, lp["log_lb"], lp["log1m_lb"], lp["hgrn_nw"], bsz, seq)
        xc = _conv(proj, lp["conv_w"], lp["conv_b"], seq)
        oc = _ssd(proj, xc, lp["dtb"], lp["arow"], lp["drow"], lp["ssm_nw"], bsz, seq)
        x2 = _outproj(x2, oa, ob, oc, lp["w_out"])
    y = _final_norm(x2, p["final_norm_w"].reshape(1, -1))
    return y.reshape(bsz, seq, D_MODEL)


def kernel(x_prompt, x_sample, norm_w, w_in, w_out, diff_lambda, diff_norm_w, hgrn_lower_bounds, hgrn_norm_w,
           conv_w, conv_b, ssm_A_log, ssm_dt_bias, ssm_D, ssm_norm_w, final_norm_w):
    p = dict(norm_w=norm_w, diff_lambda=diff_lambda, diff_norm_w=diff_norm_w, hgrn_norm_w=hgrn_norm_w,
             conv_w=conv_w, conv_b=conv_b, ssm_A_log=ssm_A_log, ssm_dt_bias=ssm_dt_bias, ssm_D=ssm_D,
             ssm_norm_w=ssm_norm_w, final_norm_w=final_norm_w)
    p["w_in_bf16"] = jnp.pad(w_in, ((0, 0), (0, 0), (0, PROJ_PAD - PROJ_WIDTH))).astype(BF16)
    p["w_out_bf16"] = w_out.astype(BF16)
    lb_all = jnp.cumsum(jax.nn.softmax(hgrn_lower_bounds.astype(F32), axis=0), axis=0)
    lb_all = lb_all - lb_all[0]
    layers = [_layer_params(p, i, lb_all) for i in range(DEPTH)]
    return (_trunk(x_prompt, p, layers), _trunk(x_sample, p, layers))
```

```python
import functools
import math

import jax
import jax.numpy as jnp
from jax import lax
from jax.experimental import pallas as pl
from jax.experimental.pallas import tpu as pltpu

F32 = jnp.float32
BF16 = jnp.bfloat16

D_MODEL = 2048
DEPTH = 4
DA_HEADS = 4
DA_DIM = 64
DA_VDIM = 128
DA_WIDTH = 512
ROPE_THETA = 500000.0
ROPE_DIMS = 16
HG_HEADS = 4
HG_DIM = 128
HG_WIDTH = 512
SSM_HEADS = 16
SSM_HEADDIM = 64
SSM_WIDTH = 1024
SSM_GROUPS = 2
SSM_HPG = 8
SSM_STATE = 128
SSM_CONV = 5
SSM_CONV_CH = 1536
MIX_WIDTH = 2048
PROJ_WIDTH = 7200
EPS = 1e-6

LANE = 128
SUBLANE = 8
PROJ_PAD = 7680
C_QA, C_KA, C_VA, C_GA = 0, 4, 8, 12
C_QB, C_FF, C_FB, C_IB, C_GB = 16, 20, 24, 28, 32
C_ZC, C_XBC, C_DT = 36, 44, 56

VMEM_LIMIT = 56 * 1024 * 1024

HG_T = 256
HG_TOP = (128, 64, 32, 16, 8)
HG_SUB = (4, 2, 1)
CONV_TL = 2048
CONV_CW = 512
SSD_T = 128
SSD_SC = 4
NEG_BIG = -1e30
LOG2E = 1.4426950408889634
ATTN_CW = 256
ATTN_KBI = 8
ATTN_AHEAD = 3


def _cparams(sem):
    return pltpu.CompilerParams(dimension_semantics=sem, vmem_limit_bytes=VMEM_LIMIT)


def _sigmoid(x):
    return 1.0 / (1.0 + jnp.exp(-x))


def _silu(x):
    return x * _sigmoid(x)


def _log1pexp_negabs(x):
    return jnp.log1p(jnp.exp(-jnp.abs(x)))


def _softplus(x):
    return jnp.maximum(x, 0.0) + _log1pexp_negabs(x)


def _dot(a, b):
    return jnp.dot(a, b, preferred_element_type=F32)


def _dot_nt(a, b):
    return lax.dot_general(a, b, (((1,), (1,)), ((), ())), preferred_element_type=F32)


def _split(x, parts):
    out = []
    r = x
    for i in range(parts):
        p = r.astype(BF16)
        out.append(p)
        if i + 1 < parts:
            r = r - p.astype(F32)
    return out


def _sel_dot(sel, x, parts=3):
    ps = _split(x, parts)
    acc = _dot(sel, ps[-1])
    for p in ps[-2::-1]:
        acc = acc + _dot(sel, p)
    return acc


def _dot_sel(x, sel, parts=3):
    ps = _split(x, parts)
    acc = _dot(ps[-1], sel)
    for p in ps[-2::-1]:
        acc = acc + _dot(p, sel)
    return acc


def _inproj_kernel(x_ref, nw_ref, w_ref, o_ref, h_scr):
    @pl.when(pl.program_id(1) == 0)
    def _():
        x = x_ref[...]
        ms = jnp.mean(x * x, axis=-1, keepdims=True)
        h_scr[...] = (x * lax.rsqrt(ms + EPS) * nw_ref[...]).astype(BF16)

    o_ref[...] = _dot(h_scr[...], w_ref[...])


def _inproj(x2, nw, w_bf16, tm=1024, tn=1536):
    m = x2.shape[0]
    tm = min(tm, m)
    return pl.pallas_call(
        _inproj_kernel,
        grid=(m // tm, PROJ_PAD // tn),
        in_specs=[
            pl.BlockSpec((tm, D_MODEL), lambda i, j: (i, 0)),
            pl.BlockSpec((1, D_MODEL), lambda i, j: (0, 0)),
            pl.BlockSpec((D_MODEL, tn), lambda i, j: (0, j)),
        ],
        out_specs=pl.BlockSpec((tm, tn), lambda i, j: (i, j)),
        out_shape=jax.ShapeDtypeStruct((m, PROJ_PAD), F32),
        scratch_shapes=[pltpu.VMEM((tm, D_MODEL), BF16)],
        compiler_params=_cparams(("parallel", "arbitrary")),
        name="inproj",
    )(x2, nw, w_bf16)


def _rope_tables(seq):
    half = ROPE_DIMS // 2
    inv = ROPE_THETA ** (-jnp.arange(0, ROPE_DIMS, 2, dtype=F32) / ROPE_DIMS)
    ang = jnp.arange(seq, dtype=F32)[:, None] * inv[None, :]
    cos, sin = jnp.cos(ang), jnp.sin(ang)
    ones = jnp.ones((seq, DA_DIM - ROPE_DIMS), F32)
    zeros = jnp.zeros((seq, DA_DIM - ROPE_DIMS), F32)
    zh = jnp.zeros((seq, half), F32)
    c64 = jnp.concatenate([cos, cos, ones], axis=1)
    s1_64 = jnp.concatenate([-sin, zh, zeros], axis=1)
    s2_64 = jnp.concatenate([zh, sin, zeros], axis=1)
    return (jnp.concatenate([c64, c64], axis=1), jnp.concatenate([s1_64, s1_64], axis=1),
            jnp.concatenate([s2_64, s2_64], axis=1))


def _prep_kernel(q_ref, k_ref, v_ref, c_ref, s1_ref, s2_ref, qo_ref, ko_ref, vo_ref):
    c, s1, s2 = c_ref[...], s1_ref[...], s2_ref[...]
    qmul = DA_DIM ** -0.5 * LOG2E
    for h in range(DA_HEADS):
        sl = slice(h * LANE, (h + 1) * LANE)
        for src, is_q in ((q_ref, True), (k_ref, False)):
            t = src[:, sl]
            r = t * c + pltpu.roll(t, LANE - ROPE_DIMS // 2, 1) * s1 + pltpu.roll(t, ROPE_DIMS // 2, 1) * s2
            if is_q:
                qo_ref[sl, :] = (r * qmul).T.astype(BF16)
            else:
                ko_ref[:, sl] = r.astype(BF16)
        vo_ref[sl, :] = v_ref[:, sl].T.astype(BF16)


def _prep(proj, tabs, seq, tr=2048):
    m = proj.shape[0]
    tr = min(tr, seq)
    nb = seq // tr
    w = DA_WIDTH
    tab_spec = pl.BlockSpec((tr, LANE), lambda i: (i % nb, 0))
    return pl.pallas_call(
        _prep_kernel,
        grid=(m // tr,),
        in_specs=[
            pl.BlockSpec((tr, w), lambda i: (i, C_QA * LANE // w)),
            pl.BlockSpec((tr, w), lambda i: (i, C_KA * LANE // w)),
            pl.BlockSpec((tr, w), lambda i: (i, C_VA * LANE // w)),
            tab_spec, tab_spec, tab_spec,
        ],
        out_specs=[pl.BlockSpec((w, tr), lambda i: (0, i)),
                   pl.BlockSpec((tr, w), lambda i: (i, 0)),
                   pl.BlockSpec((DA_HEADS * DA_VDIM, tr), lambda i: (0, i))],
        out_shape=[jax.ShapeDtypeStruct((w, m), BF16), jax.ShapeDtypeStruct((m, w), BF16),
                   jax.ShapeDtypeStruct((DA_HEADS * DA_VDIM, m), BF16)],
        compiler_params=_cparams(("parallel",)),
        name="rope_prep",
    )(proj, proj, proj, *tabs)


def _attn_kernel(lam_ref, q_ref, k_ref, v_ref, g_ref, nw_ref, o_ref, qs_scr, m_scr, l_scr, acc_scr, s_scr, *, tq, tk, nkb):
    q = q_ref[...]
    row = lax.broadcasted_iota(jnp.int32, q.shape, 0)
    zero = jnp.zeros_like(q)
    qs_scr[:, 0:tq] = jnp.where(row < DA_DIM, q, zero)
    qs_scr[:, tq:2 * tq] = jnp.where(row >= DA_DIM, q, zero)
    m_scr[...] = jnp.full(m_scr.shape, -jnp.inf, F32)
    l_scr[...] = jnp.zeros(l_scr.shape, F32)
    acc_scr[...] = jnp.zeros(acc_scr.shape, F32)
    ncc = 2 * tq // ATTN_CW

    def scores(kb, c):
        kblk = k_ref[pl.ds(pl.multiple_of(kb * tk, tk), tk), :]
        return _dot(kblk, qs_scr[:, c * ATTN_CW:(c + 1) * ATTN_CW])

    for c in range(ATTN_AHEAD):
        s_scr[c] = scores(0, c)

    def body(it, carry):
        for kk in range(kbi):
            kb = it * kbi + kk
            vt = v_ref[:, pl.ds(pl.multiple_of(kb * tk, tk), tk)]
            kb_next = jnp.minimum(kb + 1, nkb - 1)
            for c in range(ncc):
                cs = slice(c * ATTN_CW, (c + 1) * ATTN_CW)
                ca = c + ATTN_AHEAD
                s_scr[ca % ncc] = scores(kb, ca) if ca < ncc else scores(kb_next, ca - ncc)
                st = s_scr[c]
                m_prev = m_scr[:, cs]
                m_new = jnp.maximum(m_prev, jnp.max(st, axis=0, keepdims=True))
                alpha = jnp.exp2(m_prev - m_new)
                p = jnp.exp2(st - m_new)
                l_scr[:, cs] = l_scr[:, cs] * alpha + jnp.sum(p, axis=0, keepdims=True)
                acc_scr[:, cs] = acc_scr[:, cs] * alpha + _dot(vt, p.astype(BF16))
                m_scr[:, cs] = m_new
        return carry

    kbi = math.gcd(ATTN_KBI, nkb)
    lax.fori_loop(0, nkb // kbi, body, 0)

    o = acc_scr[...] / l_scr[...]
    o = (o[:, 0:tq] - lam_ref[0] * o[:, tq:2 * tq]).T
    ms = jnp.mean(o * o, axis=-1, keepdims=True)
    o = o * lax.rsqrt(ms + EPS) * nw_ref[...] * lam_ref[1]
    o_ref[...] = (o * _silu(g_ref[...])).astype(BF16)


def _attention(lamv, q_t, k_r, v_t, proj, nw, bsz, seq, tq=1024, tk=512):
    m = k_r.shape[0]
    tq, tk = min(tq, seq), min(tk, seq)
    nq, nkb = seq // tq, seq // tk
    assert 2 * tq // ATTN_CW > ATTN_AHEAD
    return pl.pallas_call(
        functools.partial(_attn_kernel, tq=tq, tk=tk, nkb=nkb),
        grid=(bsz, DA_HEADS, nq),
        in_specs=[
            pl.BlockSpec(memory_space=pltpu.SMEM),
            pl.BlockSpec((LANE, tq), lambda b, h, i: (h, b * nq + i)),
            pl.BlockSpec((seq, LANE), lambda b, h, i: (b, h)),
            pl.BlockSpec((DA_VDIM, seq), lambda b, h, i: (h, b)),
            pl.BlockSpec((tq, LANE), lambda b, h, i: (b * nq + i, C_GA + h)),
            pl.BlockSpec((1, LANE), lambda b, h, i: (0, 0)),
        ],
        out_specs=pl.BlockSpec((tq, LANE), lambda b, h, i: (b * nq + i, h)),
        out_shape=jax.ShapeDtypeStruct((m, DA_WIDTH), BF16),
        scratch_shapes=[
            pltpu.VMEM((LANE, 2 * tq), BF16),
            pltpu.VMEM((1, 2 * tq), F32),
            pltpu.VMEM((1, 2 * tq), F32),
            pltpu.VMEM((DA_VDIM, 2 * tq), F32),
            pltpu.VMEM((2 * tq // ATTN_CW, tk, ATTN_CW), F32),
        ],
        compiler_params=_cparams(("parallel", "parallel", "arbitrary")),
        name="diff_attn",
    )(lamv, q_t, k_r, v_t, proj, nw)


def _hgrn_consts(rev):
    t = HG_T
    r = jnp.arange(t)[:, None]
    c = jnp.arange(t)[None, :]
    tri = (c >= r) if rev else (c <= r)
    half = t // 2
    rr = jnp.arange(half)[:, None]
    cc = jnp.arange(half)[None, :]
    masks = [(rr // hh) == (cc // hh) for hh in HG_TOP]
    for hh in HG_SUB:
        same = (rr // (2 * hh)) == (cc // (2 * hh))
        r_hi = (rr % (2 * hh)) >= hh
        c_hi = (cc % (2 * hh)) >= hh
        masks.append(same & ((~r_hi) & c_hi if rev else r_hi & (~c_hi)))
    masks.append(rr == cc)
    return tri.astype(BF16), jnp.stack(masks).astype(F32)


def _hgrn_kernel(*refs, rev):
    if rev:
        (q_ref, f_ref, v_ref, la_ref, lc_ref, tri_ref, msk_ref,
         of_ref, gb_ref, nw_ref, o_ref, st_scr) = refs
    else:
        q_ref, f_ref, v_ref, la_ref, lc_ref, tri_ref, msk_ref, o_ref, st_scr = refs
    t = HG_T
    half = t // 2
    heads = range(HG_HEADS)
    hs = [slice(h * LANE, (h + 1) * LANE) for h in heads]

    @pl.when(pl.program_id(1) == 0)
    def _():
        st_scr[...] = jnp.zeros(st_scr.shape, F32)

    q = [_silu(q_ref[:, hs[h]]) for h in heads]
    v = [v_ref[:, hs[h]] for h in heads]
    k, cum = [], []
    tri = tri_ref[...]
    for h in heads:
        fr = f_ref[:, hs[h]]
        y = lc_ref[:, hs[h]] + (jnp.minimum(fr, 0.0) - _log1pexp_negabs(fr))
        la = la_ref[:, hs[h]]
        g = jnp.maximum(la, y) + _log1pexp_negabs(la - y)
        k.append(1.0 - jnp.exp(g))
        cum.append(_sel_dot(tri, g * LOG2E))
    tot = [c[0:1, :] if rev else c[t - 1:t, :] for c in cum]

    o = [_dot_nt((q[h] * jnp.exp2(cum[h])).astype(BF16), st_scr[h].astype(BF16)) for h in heads]

    for li, hh in enumerate(HG_TOP):
        nb = t // (2 * hh)

        def halves(x, nb=nb, hh=hh):
            x4 = x.reshape(nb, 2, hh, LANE)
            return x4[:, 0], x4[:, 1]

        for h in heads:
            c_lo, c_hi = halves(cum[h])
            q_lo, q_hi = halves(q[h])
            k_lo, k_hi = halves(k[h])
            v_lo, v_hi = halves(v[h])
            if rev:
                ref = c_hi[:, 0:1, :]
                qe, ke, vv = q_lo * jnp.exp2(c_lo - ref), k_hi * jnp.exp2(ref - c_hi), v_hi
            else:
                ref = c_lo[:, hh - 1:hh, :]
                qe, ke, vv = q_hi * jnp.exp2(c_hi - ref), k_lo * jnp.exp2(ref - c_lo), v_lo
            a = _dot_nt(qe.reshape(half, LANE).astype(BF16), ke.reshape(half, LANE).astype(BF16))
            if nb > 1:
                a = a * msk_ref[li]
            part = _dot(a.astype(BF16), vv.reshape(half, LANE).astype(BF16)).reshape(nb, hh, LANE)
            zero = jnp.zeros_like(part)
            o[h] = o[h] + jnp.stack([part, zero] if rev else [zero, part], axis=1).reshape(t, LANE)

    row8 = lax.broadcasted_iota(jnp.int32, (t, LANE), 0) & (SUBLANE - 1)
    nt = len(HG_TOP)
    qb = [x.astype(BF16) for x in q]
    kb = [x.astype(BF16) for x in k]
    vb = [x.astype(BF16) for x in v]
    asub = [[msk_ref[nt + len(HG_SUB)] * _dot_nt(qb[h][i * half:(i + 1) * half], kb[h][i * half:(i + 1) * half])
             for i in range(2)] for h in heads]
    for si, hh in enumerate(HG_SUB):
        rows = [b * 2 * hh + (hh if rev else hh - 1) for b in range(SUBLANE // (2 * hh))]
        for h in heads:
            cum3 = cum[h].reshape(t // SUBLANE, SUBLANE, LANE)

            def bcast_row(r, cum3=cum3):
                return jnp.broadcast_to(cum3[:, r:r + 1, :], cum3.shape).reshape(t, LANE)

            ref = bcast_row(rows[-1])
            for i in reversed(range(len(rows) - 1)):
                ref = jnp.where(row8 < (i + 1) * 2 * hh, bcast_row(rows[i]), ref)
            qt = (q[h] * jnp.exp2(jnp.minimum(cum[h] - ref, 0.0))).astype(BF16)
            kt = (k[h] * jnp.exp2(jnp.minimum(ref - cum[h], 0.0))).astype(BF16)
            for i in range(2):
                sl = slice(i * half, (i + 1) * half)
                asub[h][i] = asub[h][i] + msk_ref[nt + si] * _dot_nt(qt[sl], kt[sl])
    for h in heads:
        o[h] = o[h] + jnp.concatenate(
            [_dot(asub[h][i].astype(BF16), vb[h][i * half:(i + 1) * half]) for i in range(2)], axis=0)

    for h in heads:
        kh = (k[h] * jnp.exp2(tot[h] - cum[h])).astype(BF16)
        st_scr[h] = st_scr[h] * jnp.exp2(tot[h]) + _dot(v[h].T.astype(BF16), kh)

    for h in heads:
        if rev:
            oh = o[h] + of_ref[:, hs[h]]
            ms = jnp.mean(oh * oh, axis=-1, keepdims=True)
            oh = oh * lax.rsqrt(ms + EPS) * nw_ref[...]
            o_ref[:, hs[h]] = (oh * _silu(gb_ref[:, hs[h]])).astype(BF16)
        else:
            o_ref[:, hs[h]] = o[h]


def _hgrn(proj, log_lb, log1m_lb, nw, bsz, seq):
    m = proj.shape[0]
    t = HG_T
    w = HG_WIDTH
    nc = seq // t
    outs = None
    for rev in (False, True):
        tri, msk = _hgrn_consts(rev)
        if rev:
            rowmap = lambda b, c: b * nc + (nc - 1 - c)
        else:
            rowmap = lambda b, c: b * nc + c

        def col(c0, rowmap=rowmap):
            return pl.BlockSpec((t, w), lambda b, c: (rowmap(b, c), c0 * LANE // w))

        const2 = lambda shp: pl.BlockSpec(shp, lambda b, c: (0,) * len(shp))
        in_specs = [col(C_QB), col(C_FB if rev else C_FF), col(C_IB), const2((1, w)), const2((1, w)),
                    const2((t, t)), const2((len(HG_TOP) + len(HG_SUB) + 1, t // 2, t // 2))]
        args = [proj, proj, proj, log_lb, log1m_lb, tri, msk]
        if rev:
            in_specs += [col(0), col(C_GB), const2((1, LANE))]
            args += [outs, proj, nw]
        outs = pl.pallas_call(
            functools.partial(_hgrn_kernel, rev=rev),
            grid=(bsz, nc),
            in_specs=in_specs,
            out_specs=col(0),
            out_shape=jax.ShapeDtypeStruct((m, w), BF16 if rev else F32),
            scratch_shapes=[pltpu.VMEM((HG_HEADS, HG_DIM, HG_DIM), F32)],
            compiler_params=_cparams(("parallel", "arbitrary")),
            name="hgrn_bwd" if rev else "hgrn_fwd",
        )(*args)
    return outs


def _conv_kernel(x_ref, p_ref, n_ref, w_ref, b_ref, o_ref, ext_scr, *, tl, nb):
    i = pl.program_id(0)
    first = (i % nb) == 0
    last = (i % nb) == nb - 1
    halo = SUBLANE
    ext_scr[0:halo, :] = jnp.where(first, 0.0, p_ref[...])
    ext_scr[halo:halo + tl, :] = x_ref[...]
    ext_scr[halo + tl:2 * halo + tl, :] = jnp.where(last, 0.0, n_ref[...])
    n = tl + 2 * halo
    ext = ext_scr[...]
    z = [w_ref[jj:jj + 1, :] * ext for jj in range(SSM_CONV)]
    down = pltpu.roll(z[1] + pltpu.roll(z[0], 1, 0), 1, 0)
    up = pltpu.roll(z[3] + pltpu.roll(z[4], n - 1, 0), n - 1, 0)
    acc = (z[2] + down + up)[halo:halo + tl, :] + b_ref[...]
    o_ref[...] = _silu(acc)


def _conv(proj, w, b, seq, tl=CONV_TL, cw=CONV_CW):
    m = proj.shape[0]
    tl = min(tl, seq)
    nb = seq // tl
    c0 = C_XBC * LANE // cw
    r8 = tl // SUBLANE
    last8 = m // SUBLANE - 1
    return pl.pallas_call(
        functools.partial(_conv_kernel, tl=tl, nb=nb),
        grid=(m // tl, SSM_CONV_CH // cw),
        in_specs=[
            pl.BlockSpec((tl, cw), lambda i, c: (i, c0 + c)),
            pl.BlockSpec((SUBLANE, cw), lambda i, c: (jnp.maximum(i * r8 - 1, 0), c0 + c)),
            pl.BlockSpec((SUBLANE, cw), lambda i, c: (jnp.minimum((i + 1) * r8, last8), c0 + c)),
            pl.BlockSpec((SSM_CONV, cw), lambda i, c: (0, c)),
            pl.BlockSpec((1, cw), lambda i, c: (0, c)),
        ],
        out_specs=pl.BlockSpec((tl, cw), lambda i, c: (i, c)),
        out_shape=jax.ShapeDtypeStruct((m, SSM_CONV_CH), F32),
        scratch_shapes=[pltpu.VMEM((tl + 2 * SUBLANE, cw), F32)],
        compiler_params=_cparams(("parallel", "parallel")),
        name="ssm_conv",
    )(proj, proj, proj, w, b)


def _ssd_consts(rev):
    t = SSD_T
    r = jnp.arange(t)[:, None]
    c = jnp.arange(t)[None, :]
    tri = ((c >= r) if rev else (c <= r)).astype(BF16)
    d = 1 if rev else 0
    lane = jnp.arange(LANE)[:, None]
    colhead = (jnp.arange(SSM_HPG * SSM_HEADDIM) // SSM_HEADDIM)[None, :]
    xm = jnp.stack([(lane == d * SSM_HEADS + g * SSM_HPG + colhead) for g in range(SSM_GROUPS)]).astype(BF16)
    return tri, xm


def _ssd_kernel(*refs, rev):
    if rev:
        (xs_ref, bc_ref, dt_ref, dtb_ref, arow_ref, tri_ref, xm_ref,
         yf_ref, z0_ref, z1_ref, drow_ref, nw_ref, o_ref, st_scr) = refs
    else:
        xs_ref, bc_ref, dt_ref, dtb_ref, arow_ref, tri_ref, xm_ref, o_ref, st_scr = refs
    t = SSD_T
    d = 1 if rev else 0
    gw = SSM_HPG * SSM_HEADDIM

    @pl.when(pl.program_id(1) == 0)
    def _():
        st_scr[...] = jnp.zeros(st_scr.shape, F32)

    r_i = lax.broadcasted_iota(jnp.int32, (t, t), 0)
    c_i = lax.broadcasted_iota(jnp.int32, (t, t), 1)
    tmask = (c_i >= r_i) if rev else (c_i <= r_i)
    lane_lo = lax.broadcasted_iota(jnp.int32, (t, LANE), 1) < SSM_HEADDIM
    last = 0 if rev else t - 1
    tri = tri_ref[...]

    for sc in (reversed(range(SSD_SC)) if rev else range(SSD_SC)):
        rows = slice(sc * t, (sc + 1) * t)
        dt = _softplus(dt_ref[rows, :] + dtb_ref[...])
        a = dt * arow_ref[...]
        cum = _sel_dot(tri, a)
        cum_t = cum.T

        for g in range(SSM_GROUPS):
            xm = xm_ref[g]
            e_cum = _dot_sel(cum, xm, 3)
            e_dt = _dot_sel(dt, xm, 2)
            e_tot = e_cum[last:last + 1, :]
            xs = xs_ref[rows, g * gw:(g + 1) * gw]
            bm = bc_ref[rows, g * LANE:(g + 1) * LANE]
            cm = bc_ref[rows, (SSM_GROUPS + g) * LANE:(SSM_GROUPS + g + 1) * LANE]
            xd = xs * e_dt
            xdb = xd.astype(BF16)
            cmb = cm.astype(BF16)
            cb = _dot_nt(cmb, bm.astype(BF16))
            ys = []
            for p in range(SSM_HPG // 2):
                xp = xdb[:, p * LANE:(p + 1) * LANE]
                zero = jnp.zeros_like(xp)
                rhs = jnp.concatenate([jnp.where(lane_lo, xp, zero), jnp.where(lane_lo, zero, xp)], axis=0)
                ws = []
                for jj in range(2):
                    ln = d * SSM_HEADS + g * SSM_HPG + 2 * p + jj
                    diff = cum[:, ln:ln + 1] - cum_t[ln:ln + 1, :]
                    ws.append((cb * jnp.exp(jnp.where(tmask, diff, NEG_BIG))).astype(BF16))
                ys.append(_dot(jnp.concatenate(ws, axis=1), rhs))
            y = jnp.concatenate(ys, axis=1)
            y = y + _dot(cmb, st_scr[g].astype(BF16)) * jnp.exp(e_cum)
            xdec = (xd * jnp.exp(e_tot - e_cum)).astype(BF16)
            st_scr[g] = st_scr[g] * jnp.exp(e_tot) + _dot(bm.T.astype(BF16), xdec)

            sl = slice(g * gw, (g + 1) * gw)
            if rev:
                y = y + yf_ref[rows, sl] + drow_ref[:, sl] * xs
                z = (z1_ref if g else z0_ref)[rows, :]
                y = y * _silu(z)
                ms = jnp.mean(y * y, axis=-1, keepdims=True)
                o_ref[rows, sl] = (y * lax.rsqrt(ms + EPS) * nw_ref[:, sl]).astype(BF16)
            else:
                o_ref[rows, sl] = y


def _ssd(proj, xc, dtb, arow, drow, nw, bsz, seq):
    m = proj.shape[0]
    t = SSD_T
    ts = SSD_T * SSD_SC
    nc = seq // ts
    outs = None
    for rev in (False, True):
        tri, xm = _ssd_consts(rev)
        if rev:
            rowmap = lambda b, c: b * nc + (nc - 1 - c)
        else:
            rowmap = lambda b, c: b * nc + c

        def blk(width, cidx, rowmap=rowmap):
            return pl.BlockSpec((ts, width), lambda b, c: (rowmap(b, c), cidx))

        const2 = lambda shp: pl.BlockSpec(shp, lambda b, c: (0,) * len(shp))
        in_specs = [blk(SSM_WIDTH, 0), blk(512, 2), blk(LANE, C_DT), const2((1, LANE)), const2((1, LANE)),
                    const2((t, t)), const2((SSM_GROUPS, LANE, 512))]
        args = [xc, xc, proj, dtb, arow, tri, xm]
        if rev:
            zc0 = C_ZC * LANE // 512
            in_specs += [blk(SSM_WIDTH, 0), blk(512, zc0), blk(512, zc0 + 1),
                         const2((1, SSM_WIDTH)), const2((1, SSM_WIDTH))]
            args += [outs, proj, proj, drow, nw]
        outs = pl.pallas_call(
            functools.partial(_ssd_kernel, rev=rev),
            grid=(bsz, nc),
            in_specs=in_specs,
            out_specs=blk(SSM_WIDTH, 0),
            out_shape=jax.ShapeDtypeStruct((m, SSM_WIDTH), BF16 if rev else F32),
            scratch_shapes=[pltpu.VMEM((SSM_GROUPS, SSM_STATE, 512), F32)],
            compiler_params=_cparams(("parallel", "arbitrary")),
            name="ssd_bwd" if rev else "ssd_fwd",
        )(*args)
    return outs


def _outproj_kernel(*refs, final):
    if final:
        x_ref, a_ref, b_ref, c_ref, w_ref, fw_ref, o_ref = refs
    else:
        x_ref, a_ref, b_ref, c_ref, w_ref, o_ref = refs
    acc = _dot(a_ref[...], w_ref[0:DA_WIDTH, :])
    acc = acc + _dot(b_ref[...], w_ref[DA_WIDTH:DA_WIDTH + HG_WIDTH, :])
    acc = acc + _dot(c_ref[...], w_ref[DA_WIDTH + HG_WIDTH:MIX_WIDTH, :])
    y = x_ref[...] + acc
    if final:
        ms = jnp.mean(y * y, axis=-1, keepdims=True)
        y = y * lax.rsqrt(ms + EPS) * fw_ref[...]
    o_ref[...] = y


def _outproj(x2, oa, ob, oc, w_bf16, final_w=None, tm=512):
    m = x2.shape[0]
    tm = min(tm, m)
    final = final_w is not None
    in_specs = [
        pl.BlockSpec((tm, D_MODEL), lambda i: (i, 0)),
        pl.BlockSpec((tm, DA_WIDTH), lambda i: (i, 0)),
        pl.BlockSpec((tm, HG_WIDTH), lambda i: (i, 0)),
        pl.BlockSpec((tm, SSM_WIDTH), lambda i: (i, 0)),
        pl.BlockSpec((MIX_WIDTH, D_MODEL), lambda i: (0, 0)),
    ]
    args = [x2, oa, ob, oc, w_bf16]
    if final:
        in_specs.append(pl.BlockSpec((1, D_MODEL), lambda i: (0, 0)))
        args.append(final_w)
    return pl.pallas_call(
        functools.partial(_outproj_kernel, final=final),
        grid=(m // tm,),
        in_specs=in_specs,
        out_specs=pl.BlockSpec((tm, D_MODEL), lambda i: (i, 0)),
        out_shape=jax.ShapeDtypeStruct((m, D_MODEL), F32),
        compiler_params=_cparams(("parallel",)),
        name="outproj_final" if final else "outproj",
    )(*args)


def _layer_params(p, i, lb_all):
    lam_init = 0.8 - 0.6 * math.exp(-0.3 * i)
    dl = p["diff_lambda"][i].astype(F32)
    lam = jnp.exp(jnp.sum(dl[0] * dl[1])) - jnp.exp(jnp.sum(dl[2] * dl[3])) + lam_init
    lb = lb_all[i]
    pad32 = lambda v: jnp.pad(v.reshape(1, -1), ((0, 0), (0, LANE - 2 * SSM_HEADS)))
    return dict(
        norm_w=p["norm_w"][i].reshape(1, -1),
        w_in=p["w_in_bf16"][i],
        w_out=p["w_out_bf16"][i],
        lamv=jnp.stack([lam, jnp.asarray(1.0 - lam_init, F32)]).astype(F32),
        diff_nw=p["diff_norm_w"][i].reshape(1, -1),
        log_lb=jnp.log(lb).reshape(1, -1),
        log1m_lb=jnp.log1p(-lb).reshape(1, -1),
        hgrn_nw=p["hgrn_norm_w"][i].reshape(1, -1),
        conv_w=p["conv_w"][i],
        conv_b=p["conv_b"][i].reshape(1, -1),
        dtb=pad32(p["ssm_dt_bias"][i].astype(F32)),
        arow=pad32(-jnp.exp(p["ssm_A_log"][i].astype(F32))),
        drow=jnp.repeat(p["ssm_D"][i].astype(F32), SSM_HEADDIM).reshape(1, -1),
        ssm_nw=p["ssm_norm_w"][i].reshape(1, -1),
    )


def _trunk(x, p, layers):
    bsz, seq, _ = x.shape
    x2 = x.reshape(bsz * seq, D_MODEL)
    tabs = _rope_tables(seq)
    final_w = p["final_norm_w"].reshape(1, -1)
    for li, lp in enumerate(layers):
        proj = _inproj(x2, lp["norm_w"], lp["w_in"])
        q_t, k_r, v_t = _prep(proj, tabs, seq)
        oa = _attention(lp["lamv"], q_t, k_r, v_t, proj, lp["diff_nw"], bsz, seq)
        ob = _hgrn(proj, lp["log_lb"], lp["log1m_lb"], lp["hgrn_nw"], bsz, seq)
        xc = _conv(proj, lp["conv_w"], lp["conv_b"], seq)
        oc = _ssd(proj, xc, lp["dtb"], lp["arow"], lp["drow"], lp["ssm_nw"], bsz, seq)
        x2 = _outproj(x2, oa, ob, oc, lp["w_out"], final_w if li == len(layers) - 1 else None)
    return x2.reshape(bsz, seq, D_MODEL)


def kernel(x_prompt, x_sample, norm_w, w_in, w_out, diff_lambda, diff_norm_w, hgrn_lower_bounds, hgrn_norm_w,
           conv_w, conv_b, ssm_A_log, ssm_dt_bias, ssm_D, ssm_norm_w, final_norm_w):
    p = dict(norm_w=norm_w, diff_lambda=diff_lambda, diff_norm_w=diff_norm_w, hgrn_norm_w=hgrn_norm_w,
             conv_w=conv_w, conv_b=conv_b, ssm_A_log=ssm_A_log, ssm_dt_bias=ssm_dt_bias, ssm_D=ssm_D,
             ssm_norm_w=ssm_norm_w, final_norm_w=final_norm_w)
    p["w_in_bf16"] = jnp.pad(w_in, ((0, 0), (0, 0), (0, PROJ_PAD - PROJ_WIDTH))).astype(BF16)
    p["w_out_bf16"] = w_out.astype(BF16)
    lb_all = jnp.cumsum(jax.nn.softmax(hgrn_lower_bounds.astype(F32), axis=0), axis=0)
    lb_all = lb_all - lb_all[0]
    layers = [_layer_params(p, i, lb_all) for i in range(DEPTH)]
    return (_trunk(x_prompt, p, layers), _trunk(x_sample, p, layers))
```

```python
import functools
import math

import jax
import jax.numpy as jnp
from jax import lax
from jax.experimental import pallas as pl
from jax.experimental.pallas import tpu as pltpu

F32 = jnp.float32
BF16 = jnp.bfloat16

D_MODEL = 2048
DEPTH = 4
DA_HEADS = 4
DA_DIM = 64
DA_VDIM = 128
DA_WIDTH = 512
ROPE_THETA = 500000.0
ROPE_DIMS = 16
HG_HEADS = 4
HG_DIM = 128
HG_WIDTH = 512
SSM_HEADS = 16
SSM_HEADDIM = 64
SSM_WIDTH = 1024
SSM_GROUPS = 2
SSM_HPG = 8
SSM_STATE = 128
SSM_CONV = 5
SSM_CONV_CH = 1536
MIX_WIDTH = 2048
PROJ_WIDTH = 7200
EPS = 1e-6

LANE = 128
SUBLANE = 8
PROJ_PAD = 7680
C_QA, C_KA, C_VA, C_GA = 0, 4, 8, 12
C_QB, C_FF, C_FB, C_IB, C_GB = 16, 20, 24, 28, 32
C_ZC, C_XBC, C_DT = 36, 44, 56

VMEM_LIMIT = 56 * 1024 * 1024

HG_T = 256
HG_TOP = (128, 64, 32, 16, 8)
HG_SUB = (4, 2, 1)
CONV_TL = 2048
CONV_CW = 512
SSD_T = 128
SSD_SC = 4
NEG_BIG = -1e30
LOG2E = 1.4426950408889634
ATTN_CW = 256
ATTN_KBI = 8
ATTN_AHEAD = 4


def _cparams(sem):
    return pltpu.CompilerParams(dimension_semantics=sem, vmem_limit_bytes=VMEM_LIMIT)


def _sigmoid(x):
    return 1.0 / (1.0 + jnp.exp(-x))


def _silu(x):
    return x * _sigmoid(x)


def _log1pexp_negabs(x):
    return jnp.log1p(jnp.exp(-jnp.abs(x)))


def _softplus(x):
    return jnp.maximum(x, 0.0) + _log1pexp_negabs(x)


def _dot(a, b):
    return jnp.dot(a, b, preferred_element_type=F32)


def _dot_nt(a, b):
    return lax.dot_general(a, b, (((1,), (1,)), ((), ())), preferred_element_type=F32)


def _split(x, parts):
    out = []
    r = x
    for i in range(parts):
        p = r.astype(BF16)
        out.append(p)
        if i + 1 < parts:
            r = r - p.astype(F32)
    return out


def _sel_dot(sel, x, parts=3):
    ps = _split(x, parts)
    acc = _dot(sel, ps[-1])
    for p in ps[-2::-1]:
        acc = acc + _dot(sel, p)
    return acc


def _dot_sel(x, sel, parts=3):
    ps = _split(x, parts)
    acc = _dot(ps[-1], sel)
    for p in ps[-2::-1]:
        acc = acc + _dot(p, sel)
    return acc


def _inproj_kernel(x_ref, nw_ref, w_ref, o_ref, h_scr):
    @pl.when(pl.program_id(1) == 0)
    def _():
        x = x_ref[...]
        ms = jnp.mean(x * x, axis=-1, keepdims=True)
        h_scr[...] = (x * lax.rsqrt(ms + EPS) * nw_ref[...]).astype(BF16)

    o_ref[...] = _dot(h_scr[...], w_ref[...])


def _inproj(x2, nw, w_bf16, tm=1024, tn=1536):
    m = x2.shape[0]
    tm = min(tm, m)
    return pl.pallas_call(
        _inproj_kernel,
        grid=(m // tm, PROJ_PAD // tn),
        in_specs=[
            pl.BlockSpec((tm, D_MODEL), lambda i, j: (i, 0)),
            pl.BlockSpec((1, D_MODEL), lambda i, j: (0, 0)),
            pl.BlockSpec((D_MODEL, tn), lambda i, j: (0, j)),
        ],
        out_specs=pl.BlockSpec((tm, tn), lambda i, j: (i, j)),
        out_shape=jax.ShapeDtypeStruct((m, PROJ_PAD), F32),
        scratch_shapes=[pltpu.VMEM((tm, D_MODEL), BF16)],
        compiler_params=_cparams(("parallel", "arbitrary")),
        name="inproj",
    )(x2, nw, w_bf16)


def _rope_tables(seq):
    half = ROPE_DIMS // 2
    inv = ROPE_THETA ** (-jnp.arange(0, ROPE_DIMS, 2, dtype=F32) / ROPE_DIMS)
    ang = jnp.arange(seq, dtype=F32)[:, None] * inv[None, :]
    cos, sin = jnp.cos(ang), jnp.sin(ang)
    ones = jnp.ones((seq, DA_DIM - ROPE_DIMS), F32)
    zeros = jnp.zeros((seq, DA_DIM - ROPE_DIMS), F32)
    zh = jnp.zeros((seq, half), F32)
    c64 = jnp.concatenate([cos, cos, ones], axis=1)
    s1_64 = jnp.concatenate([-sin, zh, zeros], axis=1)
    s2_64 = jnp.concatenate([zh, sin, zeros], axis=1)
    return (jnp.concatenate([c64, c64], axis=1), jnp.concatenate([s1_64, s1_64], axis=1),
            jnp.concatenate([s2_64, s2_64], axis=1))


def _prep_kernel(q_ref, k_ref, v_ref, c_ref, s1_ref, s2_ref, qo_ref, ko_ref, vo_ref):
    c, s1, s2 = c_ref[...], s1_ref[...], s2_ref[...]
    qmul = DA_DIM ** -0.5 * LOG2E
    for h in range(DA_HEADS):
        sl = slice(h * LANE, (h + 1) * LANE)
        for src, is_q in ((q_ref, True), (k_ref, False)):
            t = src[:, sl]
            r = t * c + pltpu.roll(t, LANE - ROPE_DIMS // 2, 1) * s1 + pltpu.roll(t, ROPE_DIMS // 2, 1) * s2
            if is_q:
                qo_ref[sl, :] = (r * qmul).T.astype(BF16)
            else:
                ko_ref[:, sl] = r.astype(BF16)
        vo_ref[sl, :] = v_ref[:, sl].T.astype(BF16)


def _prep(proj, tabs, seq, tr=2048):
    m = proj.shape[0]
    tr = min(tr, seq)
    nb = seq // tr
    w = DA_WIDTH
    tab_spec = pl.BlockSpec((tr, LANE), lambda i: (i % nb, 0))
    return pl.pallas_call(
        _prep_kernel,
        grid=(m // tr,),
        in_specs=[
            pl.BlockSpec((tr, w), lambda i: (i, C_QA * LANE // w)),
            pl.BlockSpec((tr, w), lambda i: (i, C_KA * LANE // w)),
            pl.BlockSpec((tr, w), lambda i: (i, C_VA * LANE // w)),
            tab_spec, tab_spec, tab_spec,
        ],
        out_specs=[pl.BlockSpec((w, tr), lambda i: (0, i)),
                   pl.BlockSpec((tr, w), lambda i: (i, 0)),
                   pl.BlockSpec((DA_HEADS * DA_VDIM, tr), lambda i: (0, i))],
        out_shape=[jax.ShapeDtypeStruct((w, m), BF16), jax.ShapeDtypeStruct((m, w), BF16),
                   jax.ShapeDtypeStruct((DA_HEADS * DA_VDIM, m), BF16)],
        compiler_params=_cparams(("parallel",)),
        name="rope_prep",
    )(proj, proj, proj, *tabs)


def _attn_kernel(lam_ref, q_ref, k_ref, v_ref, g_ref, nw_ref, o_ref, qs_scr, m_scr, l_scr, acc_scr, s_scr, *, tq, tk, nkb):
    q = q_ref[...]
    row = lax.broadcasted_iota(jnp.int32, q.shape, 0)
    zero = jnp.zeros_like(q)
    qs_scr[:, 0:tq] = jnp.where(row < DA_DIM, q, zero)
    qs_scr[:, tq:2 * tq] = jnp.where(row >= DA_DIM, q, zero)
    m_scr[...] = jnp.full(m_scr.shape, -jnp.inf, F32)
    l_scr[...] = jnp.zeros(l_scr.shape, F32)
    acc_scr[...] = jnp.zeros(acc_scr.shape, F32)
    ncc = 2 * tq // ATTN_CW

    def scores(kb, c):
        kblk = k_ref[pl.ds(pl.multiple_of(kb * tk, tk), tk), :]
        return _dot(kblk, qs_scr[:, c * ATTN_CW:(c + 1) * ATTN_CW])

    for c in range(ATTN_AHEAD):
        s_scr[c] = scores(0, c)

    def body(it, carry):
        for kk in range(kbi):
            kb = it * kbi + kk
            vt = v_ref[:, pl.ds(pl.multiple_of(kb * tk, tk), tk)]
            kb_next = jnp.minimum(kb + 1, nkb - 1)
            for c in range(ncc):
                cs = slice(c * ATTN_CW, (c + 1) * ATTN_CW)
                ca = c + ATTN_AHEAD
                s_scr[ca % ncc] = scores(kb, ca) if ca < ncc else scores(kb_next, ca - ncc)
                st = s_scr[c]
                m_prev = m_scr[:, cs]
                m_new = jnp.maximum(m_prev, jnp.max(st, axis=0, keepdims=True))
                alpha = jnp.exp2(m_prev - m_new)
                p = jnp.exp2(st - m_new)
                l_scr[:, cs] = l_scr[:, cs] * alpha + jnp.sum(p, axis=0, keepdims=True)
                acc_scr[:, cs] = acc_scr[:, cs] * alpha + _dot(vt, p.astype(BF16))
                m_scr[:, cs] = m_new
        return carry

    kbi = math.gcd(ATTN_KBI, nkb)
    lax.fori_loop(0, nkb // kbi, body, 0)

    o = acc_scr[...] / l_scr[...]
    o = (o[:, 0:tq] - lam_ref[0] * o[:, tq:2 * tq]).T
    ms = jnp.mean(o * o, axis=-1, keepdims=True)
    o = o * lax.rsqrt(ms + EPS) * nw_ref[...] * lam_ref[1]
    o_ref[...] = (o * _silu(g_ref[...])).astype(BF16)


def _attention(lamv, q_t, k_r, v_t, proj, nw, bsz, seq, tq=2048, tk=512):
    m = k_r.shape[0]
    tq, tk = min(tq, seq), min(tk, seq)
    nq, nkb = seq // tq, seq // tk
    assert 2 * tq // ATTN_CW > ATTN_AHEAD
    return pl.pallas_call(
        functools.partial(_attn_kernel, tq=tq, tk=tk, nkb=nkb),
        grid=(bsz, DA_HEADS, nq),
        in_specs=[
            pl.BlockSpec(memory_space=pltpu.SMEM),
            pl.BlockSpec((LANE, tq), lambda b, h, i: (h, b * nq + i)),
            pl.BlockSpec((seq, LANE), lambda b, h, i: (b, h)),
            pl.BlockSpec((DA_VDIM, seq), lambda b, h, i: (h, b)),
            pl.BlockSpec((tq, LANE), lambda b, h, i: (b * nq + i, C_GA + h)),
            pl.BlockSpec((1, LANE), lambda b, h, i: (0, 0)),
        ],
        out_specs=pl.BlockSpec((tq, LANE), lambda b, h, i: (b * nq + i, h)),
        out_shape=jax.ShapeDtypeStruct((m, DA_WIDTH), BF16),
        scratch_shapes=[
            pltpu.VMEM((LANE, 2 * tq), BF16),
            pltpu.VMEM((1, 2 * tq), F32),
            pltpu.VMEM((1, 2 * tq), F32),
            pltpu.VMEM((DA_VDIM, 2 * tq), F32),
            pltpu.VMEM((2 * tq // ATTN_CW, tk, ATTN_CW), F32),
        ],
        compiler_params=_cparams(("parallel", "parallel", "arbitrary")),
        name="diff_attn",
    )(lamv, q_t, k_r, v_t, proj, nw)


def _hgrn_consts(rev):
    t = HG_T
    r = jnp.arange(t)[:, None]
    c = jnp.arange(t)[None, :]
    tri = (c >= r) if rev else (c <= r)
    half = t // 2
    rr = jnp.arange(half)[:, None]
    cc = jnp.arange(half)[None, :]
    masks = [(rr // hh) == (cc // hh) for hh in HG_TOP]
    for hh in HG_SUB:
        same = (rr // (2 * hh)) == (cc // (2 * hh))
        r_hi = (rr % (2 * hh)) >= hh
        c_hi = (cc % (2 * hh)) >= hh
        masks.append(same & ((~r_hi) & c_hi if rev else r_hi & (~c_hi)))
    masks.append(rr == cc)
    return tri.astype(BF16), jnp.stack(masks).astype(F32)


def _hgrn_kernel(*refs, rev):
    if rev:
        (q_ref, f_ref, v_ref, la_ref, lc_ref, tri_ref, msk_ref,
         of_ref, gb_ref, nw_ref, o_ref, st_scr) = refs
    else:
        q_ref, f_ref, v_ref, la_ref, lc_ref, tri_ref, msk_ref, o_ref, st_scr = refs
    t = HG_T
    half = t // 2
    heads = range(HG_HEADS)
    hs = [slice(h * LANE, (h + 1) * LANE) for h in heads]

    @pl.when(pl.program_id(1) == 0)
    def _():
        st_scr[...] = jnp.zeros(st_scr.shape, F32)

    q = [_silu(q_ref[:, hs[h]]) for h in heads]
    v = [v_ref[:, hs[h]] for h in heads]
    k, cum = [], []
    tri = tri_ref[...]
    for h in heads:
        fr = f_ref[:, hs[h]]
        y = lc_ref[:, hs[h]] + (jnp.minimum(fr, 0.0) - _log1pexp_negabs(fr))
        la = la_ref[:, hs[h]]
        g = jnp.maximum(la, y) + _log1pexp_negabs(la - y)
        k.append(1.0 - jnp.exp(g))
        cum.append(_sel_dot(tri, g * LOG2E))
    tot = [c[0:1, :] if rev else c[t - 1:t, :] for c in cum]

    o = [_dot_nt((q[h] * jnp.exp2(cum[h])).astype(BF16), st_scr[h].astype(BF16)) for h in heads]

    amats = []
    for li, hh in enumerate(HG_TOP):
        nb = t // (2 * hh)

        def halves(x, nb=nb, hh=hh):
            x4 = x.reshape(nb, 2, hh, LANE)
            return x4[:, 0], x4[:, 1]

        for h in heads:
            c_lo, c_hi = halves(cum[h])
            q_lo, q_hi = halves(q[h])
            k_lo, k_hi = halves(k[h])
            v_lo, v_hi = halves(v[h])
            if rev:
                ref = c_hi[:, 0:1, :]
                qe, ke, vv = q_lo * jnp.exp2(c_lo - ref), k_hi * jnp.exp2(ref - c_hi), v_hi
            else:
                ref = c_lo[:, hh - 1:hh, :]
                qe, ke, vv = q_hi * jnp.exp2(c_hi - ref), k_lo * jnp.exp2(ref - c_lo), v_lo
            a = _dot_nt(qe.reshape(half, LANE).astype(BF16), ke.reshape(half, LANE).astype(BF16))
            if nb > 1:
                a = a * msk_ref[li]
            amats.append((h, nb, hh, a.astype(BF16), vv.reshape(half, LANE).astype(BF16)))
    row8 = lax.broadcasted_iota(jnp.int32, (t, LANE), 0) & (SUBLANE - 1)
    nt = len(HG_TOP)
    qb = [x.astype(BF16) for x in q]
    kb = [x.astype(BF16) for x in k]
    vb = [x.astype(BF16) for x in v]
    asub = [[msk_ref[nt + len(HG_SUB)] * _dot_nt(qb[h][i * half:(i + 1) * half], kb[h][i * half:(i + 1) * half])
             for i in range(2)] for h in heads]
    for si, hh in enumerate(HG_SUB):
        rows = [b * 2 * hh + (hh if rev else hh - 1) for b in range(SUBLANE // (2 * hh))]
        for h in heads:
            cum3 = cum[h].reshape(t // SUBLANE, SUBLANE, LANE)

            def bcast_row(r, cum3=cum3):
                return jnp.broadcast_to(cum3[:, r:r + 1, :], cum3.shape).reshape(t, LANE)

            ref = bcast_row(rows[-1])
            for i in reversed(range(len(rows) - 1)):
                ref = jnp.where(row8 < (i + 1) * 2 * hh, bcast_row(rows[i]), ref)
            e = jnp.exp2(-jnp.abs(cum[h] - ref))
            qt = (q[h] * e).astype(BF16)
            kt = (k[h] * e).astype(BF16)
            for i in range(2):
                sl = slice(i * half, (i + 1) * half)
                asub[h][i] = asub[h][i] + msk_ref[nt + si] * _dot_nt(qt[sl], kt[sl])
    for h, nb, hh, a, vv in amats:
        part = _dot(a, vv).reshape(nb, hh, LANE)
        zero = jnp.zeros_like(part)
        o[h] = o[h] + jnp.stack([part, zero] if rev else [zero, part], axis=1).reshape(t, LANE)

    for h in heads:
        o[h] = o[h] + jnp.concatenate(
            [_dot(asub[h][i].astype(BF16), vb[h][i * half:(i + 1) * half]) for i in range(2)], axis=0)

    for h in heads:
        kh = (k[h] * jnp.exp2(tot[h] - cum[h])).astype(BF16)
        st_scr[h] = st_scr[h] * jnp.exp2(tot[h]) + _dot(v[h].T.astype(BF16), kh)

    for h in heads:
        if rev:
            oh = o[h] + of_ref[:, hs[h]]
            ms = jnp.mean(oh * oh, axis=-1, keepdims=True)
            oh = oh * lax.rsqrt(ms + EPS) * nw_ref[...]
            o_ref[:, hs[h]] = (oh * _silu(gb_ref[:, hs[h]])).astype(BF16)
        else:
            o_ref[:, hs[h]] = o[h]


def _hgrn(proj, log_lb, log1m_lb, nw, bsz, seq):
    m = proj.shape[0]
    t = HG_T
    w = HG_WIDTH
    nc = seq // t
    outs = None
    for rev in (False, True):
        tri, msk = _hgrn_consts(rev)
        if rev:
            rowmap = lambda b, c: b * nc + (nc - 1 - c)
        else:
            rowmap = lambda b, c: b * nc + c

        def col(c0, rowmap=rowmap):
            return pl.BlockSpec((t, w), lambda b, c: (rowmap(b, c), c0 * LANE // w))

        const2 = lambda shp: pl.BlockSpec(shp, lambda b, c: (0,) * len(shp))
        in_specs = [col(C_QB), col(C_FB if rev else C_FF), col(C_IB), const2((1, w)), const2((1, w)),
                    const2((t, t)), const2((len(HG_TOP) + len(HG_SUB) + 1, t // 2, t // 2))]
        args = [proj, proj, proj, log_lb, log1m_lb, tri, msk]
        if rev:
            in_specs += [col(0), col(C_GB), const2((1, LANE))]
            args += [outs, proj, nw]
        outs = pl.pallas_call(
            functools.partial(_hgrn_kernel, rev=rev),
            grid=(bsz, nc),
            in_specs=in_specs,
            out_specs=col(0),
            out_shape=jax.ShapeDtypeStruct((m, w), BF16 if rev else F32),
            scratch_shapes=[pltpu.VMEM((HG_HEADS, HG_DIM, HG_DIM), F32)],
            compiler_params=_cparams(("parallel", "arbitrary")),
            name="hgrn_bwd" if rev else "hgrn_fwd",
        )(*args)
    return outs


def _conv_kernel(x_ref, p_ref, n_ref, w_ref, b_ref, o_ref, ext_scr, *, tl, nb):
    i = pl.program_id(0)
    first = (i % nb) == 0
    last = (i % nb) == nb - 1
    halo = SUBLANE
    ext_scr[0:halo, :] = jnp.where(first, 0.0, p_ref[...])
    ext_scr[halo:halo + tl, :] = x_ref[...]
    ext_scr[halo + tl:2 * halo + tl, :] = jnp.where(last, 0.0, n_ref[...])
    n = tl + 2 * halo
    ext = ext_scr[...]
    z = [w_ref[jj:jj + 1, :] * ext for jj in range(SSM_CONV)]
    down = pltpu.roll(z[1] + pltpu.roll(z[0], 1, 0), 1, 0)
    up = pltpu.roll(z[3] + pltpu.roll(z[4], n - 1, 0), n - 1, 0)
    acc = (z[2] + down + up)[halo:halo + tl, :] + b_ref[...]
    o_ref[...] = _silu(acc)


def _conv(proj, w, b, seq, tl=CONV_TL, cw=CONV_CW):
    m = proj.shape[0]
    tl = min(tl, seq)
    nb = seq // tl
    c0 = C_XBC * LANE // cw
    r8 = tl // SUBLANE
    last8 = m // SUBLANE - 1
    return pl.pallas_call(
        functools.partial(_conv_kernel, tl=tl, nb=nb),
        grid=(m // tl, SSM_CONV_CH // cw),
        in_specs=[
            pl.BlockSpec((tl, cw), lambda i, c: (i, c0 + c)),
            pl.BlockSpec((SUBLANE, cw), lambda i, c: (jnp.maximum(i * r8 - 1, 0), c0 + c)),
            pl.BlockSpec((SUBLANE, cw), lambda i, c: (jnp.minimum((i + 1) * r8, last8), c0 + c)),
            pl.BlockSpec((SSM_CONV, cw), lambda i, c: (0, c)),
            pl.BlockSpec((1, cw), lambda i, c: (0, c)),
        ],
        out_specs=pl.BlockSpec((tl, cw), lambda i, c: (i, c)),
        out_shape=jax.ShapeDtypeStruct((m, SSM_CONV_CH), F32),
        scratch_shapes=[pltpu.VMEM((tl + 2 * SUBLANE, cw), F32)],
        compiler_params=_cparams(("parallel", "parallel")),
        name="ssm_conv",
    )(proj, proj, proj, w, b)


def _ssd_consts(rev):
    t = SSD_T
    r = jnp.arange(t)[:, None]
    c = jnp.arange(t)[None, :]
    tri = ((c >= r) if rev else (c <= r)).astype(BF16)
    d = 1 if rev else 0
    lane = jnp.arange(LANE)[:, None]
    colhead = (jnp.arange(SSM_HPG * SSM_HEADDIM) // SSM_HEADDIM)[None, :]
    xm = jnp.stack([(lane == d * SSM_HEADS + g * SSM_HPG + colhead) for g in range(SSM_GROUPS)]).astype(BF16)
    return tri, xm


def _ssd_kernel(*refs, rev):
    if rev:
        (xs_ref, bc_ref, dt_ref, dtb_ref, arow_ref, tri_ref, xm_ref,
         yf_ref, z0_ref, z1_ref, drow_ref, nw_ref, o_ref, st_scr) = refs
    else:
        xs_ref, bc_ref, dt_ref, dtb_ref, arow_ref, tri_ref, xm_ref, o_ref, st_scr = refs
    t = SSD_T
    d = 1 if rev else 0
    gw = SSM_HPG * SSM_HEADDIM

    @pl.when(pl.program_id(1) == 0)
    def _():
        st_scr[...] = jnp.zeros(st_scr.shape, F32)

    r_i = lax.broadcasted_iota(jnp.int32, (t, t), 0)
    c_i = lax.broadcasted_iota(jnp.int32, (t, t), 1)
    tmask = (c_i >= r_i) if rev else (c_i <= r_i)
    lane_lo = lax.broadcasted_iota(jnp.int32, (t, LANE), 1) < SSM_HEADDIM
    last = 0 if rev else t - 1
    tri = tri_ref[...]

    order = list(reversed(range(SSD_SC))) if rev else list(range(SSD_SC))
    rows = {sc: slice(sc * t, (sc + 1) * t) for sc in order}
    units = [(sc, g) for sc in order for g in range(SSM_GROUPS)]
    dts, cums, cum_ts = {}, {}, {}
    for sc in order:
        dts[sc] = _softplus(dt_ref[rows[sc], :] + dtb_ref[...])
        cums[sc] = _sel_dot(tri, dts[sc] * arow_ref[...])
    for sc in order:
        cum_ts[sc] = cums[sc].T

    e_cum, e_dt, cb, cmb, bmt = {}, {}, {}, {}, {}
    for u in units:
        sc, g = u
        e_cum[u] = _dot_sel(cums[sc], xm_ref[g], 3)
        e_dt[u] = _dot_sel(dts[sc], xm_ref[g], 2)
        bm = bc_ref[rows[sc], g * LANE:(g + 1) * LANE]
        cmb[u] = bc_ref[rows[sc], (SSM_GROUPS + g) * LANE:(SSM_GROUPS + g + 1) * LANE].astype(BF16)
        cb[u] = _dot_nt(cmb[u], bm.astype(BF16))
        bmt[u] = bm.T.astype(BF16)

    ydiag, upd, xs_f = {}, {}, {}
    for u in units:
        sc, g = u
        xs = xs_ref[rows[sc], g * gw:(g + 1) * gw]
        xs_f[u] = xs
        xd = xs * e_dt[u]
        xdb = xd.astype(BF16)
        e_tot = e_cum[u][last:last + 1, :]
        upd[u] = _dot(bmt[u], (xd * jnp.exp(e_tot - e_cum[u])).astype(BF16))
        ys = []
        for p in range(SSM_HPG // 2):
            xp = xdb[:, p * LANE:(p + 1) * LANE]
            zero = jnp.zeros_like(xp)
            rhs = jnp.concatenate([jnp.where(lane_lo, xp, zero), jnp.where(lane_lo, zero, xp)], axis=0)
            ws = []
            for jj in range(2):
                ln = d * SSM_HEADS + g * SSM_HPG + 2 * p + jj
                diff = cums[sc][:, ln:ln + 1] - cum_ts[sc][ln:ln + 1, :]
                ws.append((cb[u] * jnp.exp(jnp.where(tmask, diff, NEG_BIG))).astype(BF16))
            ys.append(_dot(jnp.concatenate(ws, axis=1), rhs))
        ydiag[u] = jnp.concatenate(ys, axis=1)

    st_in = {}
    for g in range(SSM_GROUPS):
        st = st_scr[g]
        for sc in order:
            u = (sc, g)
            st_in[u] = st.astype(BF16)
            st = st * jnp.exp(e_cum[u][last:last + 1, :]) + upd[u]
        st_scr[g] = st

    for u in units:
        sc, g = u
        y = ydiag[u] + _dot(cmb[u], st_in[u]) * jnp.exp(e_cum[u])
        sl = slice(g * gw, (g + 1) * gw)
        if rev:
            y = y + yf_ref[rows[sc], sl] + drow_ref[:, sl] * xs_f[u]
            z = (z1_ref if g else z0_ref)[rows[sc], :]
            y = y * _silu(z)
            ms = jnp.mean(y * y, axis=-1, keepdims=True)
            o_ref[rows[sc], sl] = (y * lax.rsqrt(ms + EPS) * nw_ref[:, sl]).astype(BF16)
        else:
            o_ref[rows[sc], sl] = y


def _ssd(proj, xc, dtb, arow, drow, nw, bsz, seq):
    m = proj.shape[0]
    t = SSD_T
    ts = SSD_T * SSD_SC
    nc = seq // ts
    outs = None
    for rev in (False, True):
        tri, xm = _ssd_consts(rev)
        if rev:
            rowmap = lambda b, c: b * nc + (nc - 1 - c)
        else:
            rowmap = lambda b, c: b * nc + c

        def blk(width, cidx, rowmap=rowmap):
            return pl.BlockSpec((ts, width), lambda b, c: (rowmap(b, c), cidx))

        const2 = lambda shp: pl.BlockSpec(shp, lambda b, c: (0,) * len(shp))
        in_specs = [blk(SSM_WIDTH, 0), blk(512, 2), blk(LANE, C_DT), const2((1, LANE)), const2((1, LANE)),
                    const2((t, t)), const2((SSM_GROUPS, LANE, 512))]
        args = [xc, xc, proj, dtb, arow, tri, xm]
        if rev:
            zc0 = C_ZC * LANE // 512
            in_specs += [blk(SSM_WIDTH, 0), blk(512, zc0), blk(512, zc0 + 1),
                         const2((1, SSM_WIDTH)), const2((1, SSM_WIDTH))]
            args += [outs, proj, proj, drow, nw]
        outs = pl.pallas_call(
            functools.partial(_ssd_kernel, rev=rev),
            grid=(bsz, nc),
            in_specs=in_specs,
            out_specs=blk(SSM_WIDTH, 0),
            out_shape=jax.ShapeDtypeStruct((m, SSM_WIDTH), BF16 if rev else F32),
            scratch_shapes=[pltpu.VMEM((SSM_GROUPS, SSM_STATE, 512), F32)],
            compiler_params=_cparams(("parallel", "arbitrary")),
            name="ssd_bwd" if rev else "ssd_fwd",
        )(*args)
    return outs


def _outproj_kernel(*refs, final):
    if final:
        x_ref, a_ref, b_ref, c_ref, w_ref, fw_ref, o_ref = refs
    else:
        x_ref, a_ref, b_ref, c_ref, w_ref, o_ref = refs
    acc = _dot(a_ref[...], w_ref[0:DA_WIDTH, :])
    acc = acc + _dot(b_ref[...], w_ref[DA_WIDTH:DA_WIDTH + HG_WIDTH, :])
    acc = acc + _dot(c_ref[...], w_ref[DA_WIDTH + HG_WIDTH:MIX_WIDTH, :])
    y = x_ref[...] + acc
    if final:
        ms = jnp.mean(y * y, axis=-1, keepdims=True)
        y = y * lax.rsqrt(ms + EPS) * fw_ref[...]
    o_ref[...] = y


def _outproj(x2, oa, ob, oc, w_bf16, final_w=None, tm=512):
    m = x2.shape[0]
    tm = min(tm, m)
    final = final_w is not None
    in_specs = [
        pl.BlockSpec((tm, D_MODEL), lambda i: (i, 0)),
        pl.BlockSpec((tm, DA_WIDTH), lambda i: (i, 0)),
        pl.BlockSpec((tm, HG_WIDTH), lambda i: (i, 0)),
        pl.BlockSpec((tm, SSM_WIDTH), lambda i: (i, 0)),
        pl.BlockSpec((MIX_WIDTH, D_MODEL), lambda i: (0, 0)),
    ]
    args = [x2, oa, ob, oc, w_bf16]
    if final:
        in_specs.append(pl.BlockSpec((1, D_MODEL), lambda i: (0, 0)))
        args.append(final_w)
    return pl.pallas_call(
        functools.partial(_outproj_kernel, final=final),
        grid=(m // tm,),
        in_specs=in_specs,
        out_specs=pl.BlockSpec((tm, D_MODEL), lambda i: (i, 0)),
        out_shape=jax.ShapeDtypeStruct((m, D_MODEL), F32),
        compiler_params=_cparams(("parallel",)),
        name="outproj_final" if final else "outproj",
    )(*args)


def _layer_params(p, i, lb_all):
    lam_init = 0.8 - 0.6 * math.exp(-0.3 * i)
    dl = p["diff_lambda"][i].astype(F32)
    lam = jnp.exp(jnp.sum(dl[0] * dl[1])) - jnp.exp(jnp.sum(dl[2] * dl[3])) + lam_init
    lb = lb_all[i]
    pad32 = lambda v: jnp.pad(v.reshape(1, -1), ((0, 0), (0, LANE - 2 * SSM_HEADS)))
    return dict(
        norm_w=p["norm_w"][i].reshape(1, -1),
        w_in=p["w_in_bf16"][i],
        w_out=p["w_out_bf16"][i],
        lamv=jnp.stack([lam, jnp.asarray(1.0 - lam_init, F32)]).astype(F32),
        diff_nw=p["diff_norm_w"][i].reshape(1, -1),
        log_lb=jnp.log(lb).reshape(1, -1),
        log1m_lb=jnp.log1p(-lb).reshape(1, -1),
        hgrn_nw=p["hgrn_norm_w"][i].reshape(1, -1),
        conv_w=p["conv_w"][i],
        conv_b=p["conv_b"][i].reshape(1, -1),
        dtb=pad32(p["ssm_dt_bias"][i].astype(F32)),
        arow=pad32(-jnp.exp(p["ssm_A_log"][i].astype(F32))),
        drow=jnp.repeat(p["ssm_D"][i].astype(F32), SSM_HEADDIM).reshape(1, -1),
        ssm_nw=p["ssm_norm_w"][i].reshape(1, -1),
    )


def _trunk(x, p, layers):
    bsz, seq, _ = x.shape
    x2 = x.reshape(bsz * seq, D_MODEL)
    tabs = _rope_tables(seq)
    final_w = p["final_norm_w"].reshape(1, -1)
    for li, lp in enumerate(layers):
        proj = _inproj(x2, lp["norm_w"], lp["w_in"])
        q_t, k_r, v_t = _prep(proj, tabs, seq)
        oa = _attention(lp["lamv"], q_t, k_r, v_t, proj, lp["diff_nw"], bsz, seq)
        ob = _hgrn(proj, lp["log_lb"], lp["log1m_lb"], lp["hgrn_nw"], bsz, seq)
        xc = _conv(proj, lp["conv_w"], lp["conv_b"], seq)
        oc = _ssd(proj, xc, lp["dtb"], lp["arow"], lp["drow"], lp["ssm_nw"], bsz, seq)
        x2 = _outproj(x2, oa, ob, oc, lp["w_out"], final_w if li == len(layers) - 1 else None)
    return x2.reshape(bsz, seq, D_MODEL)


def kernel(x_prompt, x_sample, norm_w, w_in, w_out, diff_lambda, diff_norm_w, hgrn_lower_bounds, hgrn_norm_w,
           conv_w, conv_b, ssm_A_log, ssm_dt_bias, ssm_D, ssm_norm_w, final_norm_w):
    p = dict(norm_w=norm_w, diff_lambda=diff_lambda, diff_norm_w=diff_norm_w, hgrn_norm_w=hgrn_norm_w,
             conv_w=conv_w, conv_b=conv_b, ssm_A_log=ssm_A_log, ssm_dt_bias=ssm_dt_bias, ssm_D=ssm_D,
             ssm_norm_w=ssm_norm_w, final_norm_w=final_norm_w)
    p["w_in_bf16"] = jnp.pad(w_in, ((0, 0), (0, 0), (0, PROJ_PAD - PROJ_WIDTH))).astype(BF16)
    p["w_out_bf16"] = w_out.astype(BF16)
    lb_all = jnp.cumsum(jax.nn.softmax(hgrn_lower_bounds.astype(F32), axis=0), axis=0)
    lb_all = lb_all - lb_all[0]
    layers = [_layer_params(p, i, lb_all) for i in range(DEPTH)]
    return (_trunk(x_prompt, p, layers), _trunk(x_sample, p, layers))
```
